```python
import math
import jax, jax.numpy as jnp
from jax import lax
import numpy as np

D_MODEL = 1024
BATCH = 8
SEQ = 4096
DEPTH = 4

N_A = DEPTH // 2
N_B = DEPTH - N_A
N_VRES = max(N_A - 1, 0)
RWKV_HEAD = 64
RWKV_HEADS = D_MODEL // RWKV_HEAD
DECAY_LORA = 64
AAA_LORA = 64
MV_LORA = 32
GATE_LORA = 160
GN_EPS = 64e-5
DIFF_HEAD = 64
DIFF_HEADS = D_MODEL // (2 * DIFF_HEAD)
QK_WIDTH = DIFF_HEADS * 2 * DIFF_HEAD
V_WIDTH = DIFF_HEADS * 2 * DIFF_HEAD
ROT_DIM = DIFF_HEAD // 4
ROPE_THETA = 500000.0
Q_BLOCK = 128
D_FF = ((8 * D_MODEL // 3 + 127) // 128) * 128
CONV_W = 3
NORM_EPS = 1e-6

kernel_name = "yoco_rwkv7_diffattn_convglu_adaln"


def rms_norm(x, g):
    xf = x.astype(jnp.float32)
    y = xf * lax.rsqrt(jnp.mean(xf * xf, axis=-1, keepdims=True) + NORM_EPS)
    return (y * g.astype(jnp.float32)).astype(x.dtype)


def modulate(h, shift, scale):
    return h * (1.0 + scale[:, None, :]) + shift[:, None, :]


def rope_tables(seq):
    pos = jnp.arange(seq, dtype=jnp.float32)
    inv = ROPE_THETA ** (-jnp.arange(0, ROT_DIM, 2, dtype=jnp.float32) / ROT_DIM)
    ang = pos[:, None] * inv[None, :]
    return jnp.cos(ang), jnp.sin(ang)


def partial_rope(t, cos, sin):
    half = ROT_DIM // 2
    cs = cos[None, :, None, None, :].astype(t.dtype)
    sn = sin[None, :, None, None, :].astype(t.dtype)
    x1 = t[..., :half]
    x2 = t[..., half:ROT_DIM]
    return jnp.concatenate([x1 * cs - x2 * sn, x2 * cs + x1 * sn, t[..., ROT_DIM:]], axis=-1)


def conv_glu_ffn(h, w_up, conv_w, conv_b, w_down):
    s = h.shape[1]
    u = h @ w_up
    up = jnp.pad(u, ((0, 0), (CONV_W - 1, 0), (0, 0)))
    u = conv_b + conv_w[0] * up[:, 0:s]
    for j in range(1, CONV_W):
        u = u + conv_w[j] * up[:, j:j + s]
    gate, val = jnp.split(u, 2, axis=-1)
    return (jax.nn.silu(gate) * val) @ w_down


def wkv7_scan(r, w, k, v, a_in, b_in):
    bsz, _, nh, n = r.shape

    def step(state, inp):
        rt, wt, kt, vt, at, bt = inp
        sa = jnp.einsum('bhvk,bhk->bhv', state, at)
        state = state * wt[:, :, None, :] + sa[..., None] * bt[:, :, None, :] + vt[..., None] * kt[:, :, None, :]
        y = jnp.einsum('bhvk,bhk->bhv', state, rt)
        return state, y

    xs = (jnp.moveaxis(r, 1, 0), jnp.moveaxis(w, 1, 0), jnp.moveaxis(k, 1, 0),
          jnp.moveaxis(v, 1, 0), jnp.moveaxis(a_in, 1, 0), jnp.moveaxis(b_in, 1, 0))
    s0 = jnp.zeros((bsz, nh, n, n), jnp.float32)
    _, ys = lax.scan(step, s0, xs)
    return jnp.moveaxis(ys, 0, 1)


def rwkv7_time_mix(h, v_first, vres, mu, w_rkv, w0, w1, w2, a0, a1, a2, g1, g2, k_k, k_a, r_k, ln_w, ln_b, w_o):
    bsz, s, d = h.shape
    xx = jnp.pad(h, ((0, 0), (1, 0), (0, 0)))[:, :-1] - h
    xr = h + xx * mu[0]
    xw = h + xx * mu[1]
    xk = h + xx * mu[2]
    xv = h + xx * mu[3]
    xa = h + xx * mu[4]
    xg = h + xx * mu[5]
    r = xr @ w_rkv[0]
    k = xk @ w_rkv[1]
    v = xv @ w_rkv[2]
    w = -jax.nn.softplus(-(w0 + jnp.tanh(xw @ w1) @ w2)) - 0.5
    if vres is None:
        v_first = v
    else:
        v0, v1, v2 = vres
        v = v + (v_first - v) * jax.nn.sigmoid(v0 + (xv @ v1) @ v2)
    a = jax.nn.sigmoid(a0 + (xa @ a1) @ a2)
    g = jax.nn.sigmoid(xg @ g1) @ g2

    def heads(t):
        return t.reshape(bsz, s, RWKV_HEADS, RWKV_HEAD).astype(jnp.float32)

    kk = heads(k * k_k)
    kk = kk / jnp.maximum(jnp.sqrt(jnp.sum(kk * kk, axis=-1, keepdims=True)), 1e-12)
    k = k * (1.0 + (a - 1.0) * k_a)
    rh, kh, vh, ah = heads(r), heads(k), heads(v), heads(a)
    decay = jnp.exp(-jnp.exp(heads(w)))
    o = wkv7_scan(rh, decay, kh, vh, -kk, kk * ah)
    mean = jnp.mean(o, axis=-1, keepdims=True)
    var = jnp.mean(jnp.square(o - mean), axis=-1, keepdims=True)
    o = ((o - mean) * lax.rsqrt(var + GN_EPS)).reshape(bsz, s, d) * ln_w + ln_b
    bonus = jnp.sum(rh * kh * r_k.astype(jnp.float32), axis=-1, keepdims=True) * vh
    o = o + bonus.reshape(bsz, s, d)
    return (o * g).astype(h.dtype) @ w_o, v_first


def diff_attention(h, k_sh, v_sh, cos, sin, w_q, lam_vecs, subln, w_o, lam_init):
    bsz, s, d = h.shape
    q = (h @ w_q).reshape(bsz, s, DIFF_HEADS, 2, DIFF_HEAD)
    q = partial_rope(q, cos, sin) * (DIFF_HEAD ** -0.5)
    lv = lam_vecs.astype(jnp.float32)
    lam = jnp.exp(jnp.sum(lv[0] * lv[1])) - jnp.exp(jnp.sum(lv[2] * lv[3])) + lam_init
    nqb = s // Q_BLOCK
    qb = q.reshape(bsz, nqb, Q_BLOCK, DIFF_HEADS, 2, DIFF_HEAD).transpose(1, 0, 2, 3, 4, 5)
    kpos = jnp.arange(s)

    def block(args):
        qblk, bi = args
        sc = jnp.einsum('bqhcd,bkhcd->bhcqk', qblk, k_sh, preferred_element_type=jnp.float32)
        qpos = bi * Q_BLOCK + jnp.arange(Q_BLOCK)
        mask = kpos[None, :] <= qpos[:, None]
        sc = jnp.where(mask, sc, -jnp.inf)
        p = jax.nn.softmax(sc, axis=-1)
        attn = p[:, :, 0] - lam * p[:, :, 1]
        return jnp.einsum('bhqk,bkhe->bqhe', attn.astype(v_sh.dtype), v_sh)

    o = lax.map(block, (qb, jnp.arange(nqb)))
    o = o.transpose(1, 0, 2, 3, 4).reshape(bsz, s, DIFF_HEADS, 2 * DIFF_HEAD)
    o = rms_norm(o, subln) * (1.0 - lam_init)
    return o.reshape(bsz, s, d) @ w_o


def setup_inputs(seed: int = 0) -> dict:
    key = jax.random.key(seed)
    ks = iter(jax.random.split(key, 40))
    D = D_MODEL
    F2 = 2 * D_FF

    def nrm(shape, scale):
        return jax.random.normal(next(ks), shape, jnp.float32) * scale

    x = nrm((BATCH, SEQ, D), 1.0)
    c = nrm((BATCH, D), 1.0)
    ada_w = nrm((DEPTH, D, 6 * D), 0.5 * D ** -0.5)
    ada_b = nrm((DEPTH, 6 * D), 0.02)
    norm1 = 1.0 + nrm((DEPTH, D), 0.02)
    norm2 = 1.0 + nrm((DEPTH, D), 0.02)
    final_norm = 1.0 + nrm((D,), 0.02)
    a_mu = jax.random.uniform(next(ks), (N_A, 6, D), jnp.float32, 0.0, 1.0)
    a_w_rkv = nrm((N_A, 3, D, D), D ** -0.5)
    a_w0 = jax.random.uniform(next(ks), (N_A, D), jnp.float32, -2.5, 0.5)
    a_w1 = nrm((N_A, D, DECAY_LORA), D ** -0.5)
    a_w2 = nrm((N_A, DECAY_LORA, D), 0.3 * DECAY_LORA ** -0.5)
    a_a0 = nrm((N_A, D), 0.5)
    a_a1 = nrm((N_A, D, AAA_LORA), D ** -0.5)
    a_a2 = nrm((N_A, AAA_LORA, D), 0.3 * AAA_LORA ** -0.5)
    a_v0 = nrm((N_VRES, D), 0.5)
    a_v1 = nrm((N_VRES, D, MV_LORA), D ** -0.5)
    a_v2 = nrm((N_VRES, MV_LORA, D), 0.3 * MV_LORA ** -0.5)
    a_g1 = nrm((N_A, D, GATE_LORA), D ** -0.5)
    a_g2 = nrm((N_A, GATE_LORA, D), GATE_LORA ** -0.5)
    a_k_k = 0.85 + nrm((N_A, D), 0.1)
    a_k_a = 1.0 + nrm((N_A, D), 0.1)
    a_r_k = nrm((N_A, RWKV_HEADS, RWKV_HEAD), 0.1)
    a_ln_w = 1.0 + nrm((N_A, D), 0.02)
    a_ln_b = nrm((N_A, D), 0.02)
    a_w_o = nrm((N_A, D, D), D ** -0.5)
    kv_norm = 1.0 + nrm((D,), 0.02)
    w_kv = nrm((D, QK_WIDTH + V_WIDTH), D ** -0.5)
    b_w_q = nrm((N_B, D, QK_WIDTH), D ** -0.5)
    b_lam = nrm((N_B, 4, DIFF_HEAD), 0.1)
    b_subln = 1.0 + nrm((N_B, 2 * DIFF_HEAD), 0.02)
    b_w_o = nrm((N_B, V_WIDTH, D), V_WIDTH ** -0.5)
    ffn_w_up = nrm((DEPTH, D, F2), D ** -0.5)
    ffn_conv_w = nrm((DEPTH, CONV_W, F2), CONV_W ** -0.5)
    ffn_conv_b = nrm((DEPTH, F2), 0.02)
    ffn_w_down = nrm((DEPTH, D_FF, D), D_FF ** -0.5)
    return {"x": x, "c": c, "ada_w": ada_w, "ada_b": ada_b, "norm1": norm1, "norm2": norm2,
            "final_norm": final_norm, "a_mu": a_mu, "a_w_rkv": a_w_rkv, "a_w0": a_w0, "a_w1": a_w1,
            "a_w2": a_w2, "a_a0": a_a0, "a_a1": a_a1, "a_a2": a_a2, "a_v0": a_v0, "a_v1": a_v1,
            "a_v2": a_v2, "a_g1": a_g1, "a_g2": a_g2, "a_k_k": a_k_k, "a_k_a": a_k_a, "a_r_k": a_r_k,
            "a_ln_w": a_ln_w, "a_ln_b": a_ln_b, "a_w_o": a_w_o, "kv_norm": kv_norm, "w_kv": w_kv,
            "b_w_q": b_w_q, "b_lam": b_lam, "b_subln": b_subln, "b_w_o": b_w_o,
            "ffn_w_up": ffn_w_up, "ffn_conv_w": ffn_conv_w, "ffn_conv_b": ffn_conv_b,
            "ffn_w_down": ffn_w_down}


def reference(x, c, ada_w, ada_b, norm1, norm2, final_norm,
              a_mu, a_w_rkv, a_w0, a_w1, a_w2, a_a0, a_a1, a_a2, a_v0, a_v1, a_v2,
              a_g1, a_g2, a_k_k, a_k_a, a_r_k, a_ln_w, a_ln_b, a_w_o,
              kv_norm, w_kv, b_w_q, b_lam, b_subln, b_w_o,
              ffn_w_up, ffn_conv_w, ffn_conv_b, ffn_w_down):
    bsz, s, d = x.shape
    cos, sin = rope_tables(s)
    c_act = jax.nn.silu(c)
    v_first = None
    k_sh = None
    v_sh = None
    for l in range(DEPTH):
        mod = c_act @ ada_w[l] + ada_b[l]
        sh1, sc1, g1, sh2, sc2, g2 = jnp.split(mod, 6, axis=-1)
        h = modulate(rms_norm(x, norm1[l]), sh1, sc1)
        if l < N_A:
            vres = None if l == 0 else (a_v0[l - 1], a_v1[l - 1], a_v2[l - 1])
            y, v_first = rwkv7_time_mix(h, v_first, vres, a_mu[l], a_w_rkv[l], a_w0[l], a_w1[l], a_w2[l],
                                        a_a0[l], a_a1[l], a_a2[l], a_g1[l], a_g2[l], a_k_k[l], a_k_a[l],
                                        a_r_k[l], a_ln_w[l], a_ln_b[l], a_w_o[l])
        else:
            j = l - N_A
            if j == 0:
                kv = rms_norm(x, kv_norm) @ w_kv
                k_sh = partial_rope(kv[..., :QK_WIDTH].reshape(bsz, s, DIFF_HEADS, 2, DIFF_HEAD), cos, sin)
                v_sh = kv[..., QK_WIDTH:].reshape(bsz, s, DIFF_HEADS, 2 * DIFF_HEAD)
            lam_init = 0.8 - 0.6 * math.exp(-0.3 * l)
            y = diff_attention(h, k_sh, v_sh, cos, sin, b_w_q[j], b_lam[j], b_subln[j], b_w_o[j], lam_init)
        x = x + g1[:, None, :] * y
        h2 = modulate(rms_norm(x, norm2[l]), sh2, sc2)
        x = x + g2[:, None, :] * conv_glu_ffn(h2, ffn_w_up[l], ffn_conv_w[l], ffn_conv_b[l], ffn_w_down[l])
    return rms_norm(x, final_norm)
```

```python
import functools
import math

import jax
import jax.numpy as jnp
from jax import lax
from jax.experimental import pallas as pl
from jax.experimental.pallas import tpu as pltpu

F32 = jnp.float32
BF16 = jnp.bfloat16
HIGHEST = lax.Precision.HIGHEST

NORM_EPS = 1e-6
GN_EPS = 64e-5
ROPE_THETA = 500000.0
HEAD = 64
ROT_DIM = HEAD // 4
LANES = 128
WKV_CHUNK = 64
WKV_HEADS_PER_GROUP = LANES // HEAD
VMEM_LIMIT = 56 * 1024 * 1024


def _tiles(seq):
    def fit(t):
        t = min(t, seq)
        assert seq % t == 0
        return t
    return dict(proj=fit(256), wkv=fit(256), res=fit(512), ffn=fit(512), qkv=fit(512), attn=fit(512),
                fin=fit(512))


def _cparams(*sem):
    return pltpu.CompilerParams(dimension_semantics=sem, vmem_limit_bytes=VMEM_LIMIT)


def _dot(a, b, **kw):
    return jnp.dot(a, b, preferred_element_type=F32, **kw)


def _dot_nt(a, b):
    return lax.dot_general(a, b, (((1,), (1,)), ((), ())), preferred_element_type=F32)


def _dot_tn(a, b):
    return lax.dot_general(a, b, (((0,), (0,)), ((), ())), preferred_element_type=F32)


def _rms(x, g):
    return x * lax.rsqrt(jnp.mean(x * x, axis=-1, keepdims=True) + NORM_EPS) * g


def _sigmoid(x):
    return 1.0 / (1.0 + jnp.exp(-x))


def _softplus(x):
    return jnp.maximum(x, 0.0) + jnp.log1p(jnp.exp(-jnp.abs(x)))


def _mod_kernel(c_ref, w_ref, b_ref, o_ref):
    c = c_ref[...]
    o_ref[0] = _dot(c * _sigmoid(c), w_ref[0], precision=HIGHEST) + b_ref[0]


def _ada_mod(c, ada_w, ada_b):
    depth, d, n = ada_w.shape
    bsz = c.shape[0]
    tn = d
    return pl.pallas_call(
        _mod_kernel,
        grid=(depth, n // tn),
        in_specs=[pl.BlockSpec((bsz, d), lambda l, j: (0, 0)),
                  pl.BlockSpec((1, d, tn), lambda l, j: (l, 0, j)),
                  pl.BlockSpec((1, 1, tn), lambda l, j: (l, 0, j))],
        out_specs=pl.BlockSpec((1, bsz, tn), lambda l, j: (l, 0, j)),
        out_shape=jax.ShapeDtypeStruct((depth, bsz, n), F32),
        compiler_params=_cparams("parallel", "parallel"),
        name="ada_mod",
    )(c, ada_w, ada_b.reshape(depth, 1, n))


def _rwkv_proj_kernel(has_vres, x_ref, xh_ref, nw_ref, sh_ref, sc_ref, mu_ref, wrkv_ref,
                      w0_ref, w1_ref, w2_ref, a0_ref, a1_ref, a2_ref, g1_ref, g2_ref, *rest):
    if has_vres:
        v0_ref, v1_ref, v2_ref, vf_ref = rest[:4]
        rest = rest[4:]
    r_out, k_out, v_out, ld_out, a_out, g_out = rest
    i = pl.program_id(1)
    nw, sh, sc = nw_ref[...], sh_ref[0], sc_ref[0]

    def prep(xv):
        return _rms(xv, nw) * (1.0 + sc) + sh

    h = prep(x_ref[0])
    halo = prep(xh_ref[0])
    prev_last = jnp.where(i > 0, halo[7:8, :], 0.0)
    row = lax.broadcasted_iota(jnp.int32, h.shape, 0)
    hprev = jnp.where(row == 0, prev_last, pltpu.roll(h, 1, axis=0))
    xx = hprev - h

    def mix(j):
        return (h + xx * mu_ref[j:j + 1, :]).astype(BF16)

    xr, xw, xk, xv, xa, xg = (mix(j) for j in range(6))
    r_out[0] = _dot(xr, wrkv_ref[0])
    k_out[0] = _dot(xk, wrkv_ref[1])
    v = _dot(xv, wrkv_ref[2])
    if has_vres:
        gate = _sigmoid(v0_ref[...] + _dot(_dot(xv, v1_ref[...]).astype(BF16), v2_ref[...]))
        v = v + (vf_ref[0] - v) * gate
    v_out[0] = v
    wl = w0_ref[...] + _dot(jnp.tanh(_dot(xw, w1_ref[...])).astype(BF16), w2_ref[...])
    ld_out[0] = -jnp.exp(-_softplus(-wl) - 0.5)
    a_out[0] = _sigmoid(a0_ref[...] + _dot(_dot(xa, a1_ref[...]).astype(BF16), a2_ref[...]))
    g_out[0] = _dot(_sigmoid(_dot(xg, g1_ref[...])).astype(BF16), g2_ref[...]).astype(g_out.dtype)


def _rwkv_proj(x, nw, sh, sc, mu, wrkv, w0, w1, w2, a0, a1, a2, g1, g2, vres, ts):
    bsz, seq, d = x.shape
    has_vres = vres is not None
    row = lambda a: a.reshape(1, -1)
    full = lambda a: pl.BlockSpec(a.shape, lambda b, i: (0,) * a.ndim)
    tile = pl.BlockSpec((1, ts, d), lambda b, i: (b, i, 0))
    halo = pl.BlockSpec((1, 8, d), lambda b, i: (b, jnp.maximum(i * (ts // 8) - 1, 0), 0))
    per_b = pl.BlockSpec((1, 1, d), lambda b, i: (b, 0, 0))
    args = [x, x, row(nw), sh, sc, mu, wrkv, row(w0), w1, w2, row(a0), a1, a2, g1, g2]
    specs = [tile, halo, full(row(nw)), per_b, per_b, full(mu), full(wrkv), full(row(w0)), full(w1),
             full(w2), full(row(a0)), full(a1), full(a2), full(g1), full(g2)]
    if has_vres:
        v0, v1, v2, vf = vres
        args += [row(v0), v1, v2, vf]
        specs += [full(row(v0)), full(v1), full(v2), tile]
    out = jax.ShapeDtypeStruct((bsz, seq, d), F32)
    return pl.pallas_call(
        functools.partial(_rwkv_proj_kernel, has_vres),
        grid=(bsz, seq // ts),
        in_specs=specs,
        out_specs=[tile] * 6,
        out_shape=[out] * 5 + [jax.ShapeDtypeStruct((bsz, seq, d), BF16)],
        compiler_params=_cparams("parallel", "arbitrary"),
        name="rwkv_proj",
    )(*args)


def _wkv_chunk(r, k, v, ld, a, g, k_k, k_a, r_k, ln_w, ln_b, state):
    c, w = r.shape
    heads = w // HEAD
    rows = heads * c
    lane_head = lax.broadcasted_iota(jnp.int32, (c, w), 1) // HEAD

    def stack(x):
        return jnp.concatenate([jnp.where(lane_head == hd, x, 0.0) for hd in range(heads)], axis=0)

    ti = lax.broadcasted_iota(jnp.int32, (c, c), 0)
    tj = lax.broadcasted_iota(jnp.int32, (c, c), 1)
    tril = (ti >= tj).astype(F32)
    si = lax.broadcasted_iota(jnp.int32, (w, w), 0) // HEAD
    sj = lax.broadcasted_iota(jnp.int32, (w, w), 1) // HEAD
    seg = (si == sj).astype(F32)
    ri = lax.broadcasted_iota(jnp.int32, (rows, rows), 0)
    rj = lax.broadcasted_iota(jnp.int32, (rows, rows), 1)
    same = (ri // c) == (rj // c)
    strict = same & ((rj % c) < (ri % c))
    incl = same & ((rj % c) <= (ri % c))
    eye = (ri == rj).astype(F32)

    def segsum(x):
        return _dot(x, seg, precision=HIGHEST)

    cum = _dot(tril, ld, precision=HIGHEST)
    last = cum[c - 1:c, :]
    g_in, g_ex, g_inv, g_end = jnp.exp(cum), jnp.exp(cum - ld), jnp.exp(-cum), jnp.exp(last - cum)

    kk = k * k_k
    kk = kk / jnp.maximum(jnp.sqrt(segsum(kk * kk)), 1e-12)
    k2 = k * (1.0 + (a - 1.0) * k_a)
    b_in = kk * a
    lhs = jnp.concatenate([stack(-kk * g_ex), stack(r * g_in)], axis=0).astype(BF16)
    rhs = jnp.concatenate([stack(b_in * g_inv), stack(k2 * g_inv)], axis=0).astype(BF16)
    vs = stack(v)

    aa = _dot_nt(lhs, rhs)
    a_ab = jnp.where(strict, aa[:rows, :rows], 0.0)
    a_ak = jnp.where(strict, aa[:rows, rows:], 0.0)
    a_r = jnp.where(jnp.concatenate([incl, incl], axis=1), aa[rows:, :], 0.0)
    p = _dot_nt(lhs, state.astype(BF16))

    t_inv = eye + a_ab
    n_pow = a_ab
    for _ in range(int(math.log2(c)) - 1):
        n_pow = _dot(n_pow, n_pow, precision=HIGHEST)
        t_inv = t_inv + _dot(n_pow, t_inv, precision=HIGHEST)
    u = _dot(t_inv, p[:rows] + _dot(a_ak.astype(BF16), vs.astype(BF16)), precision=HIGHEST)

    uv = jnp.concatenate([u, vs], axis=0).astype(BF16)
    y_st = p[rows:] + _dot(a_r.astype(BF16), uv)
    y = y_st[:c]
    for hd in range(1, heads):
        y = y + y_st[hd * c:(hd + 1) * c]
    bk_end = jnp.concatenate([stack(b_in * g_end), stack(k2 * g_end)], axis=0).astype(BF16)
    new_state = state * jnp.exp(last) + _dot_tn(uv, bk_end)

    mean = segsum(y) * (1.0 / HEAD)
    dlt = y - mean
    var = segsum(dlt * dlt) * (1.0 / HEAD)
    out = dlt * lax.rsqrt(var + GN_EPS) * ln_w + ln_b + segsum(r * k2 * r_k) * v
    return out * g, new_state


def _wkv_kernel(groups, r_ref, k_ref, v_ref, ld_ref, a_ref, g_ref, kk_ref, ka_ref, rk_ref, lnw_ref, lnb_ref,
                o_ref, s_ref):
    @pl.when(pl.program_id(2) == 0)
    def _():
        s_ref[...] = jnp.zeros_like(s_ref)

    n_chunks = r_ref.shape[1] // WKV_CHUNK

    def body(ci, carry):
        tok = pl.ds(pl.multiple_of(ci * WKV_CHUNK, WKV_CHUNK), WKV_CHUNK)
        for gi in range(groups):
            ln = pl.ds(gi * LANES, LANES)
            out, new_state = _wkv_chunk(
                r_ref[0, tok, ln], k_ref[0, tok, ln], v_ref[0, tok, ln], ld_ref[0, tok, ln], a_ref[0, tok, ln],
                g_ref[0, tok, ln].astype(F32), kk_ref[:, ln], ka_ref[:, ln], rk_ref[:, ln], lnw_ref[:, ln],
                lnb_ref[:, ln], s_ref[gi])
            s_ref[gi] = new_state
            o_ref[0, tok, ln] = out.astype(o_ref.dtype)
        return carry

    lax.fori_loop(0, n_chunks, body, 0)


def _wkv(r, k, v, ld, a, g, k_k, k_a, r_k, ln_w, ln_b, ts, groups=4):
    bsz, seq, d = r.shape
    wid = groups * LANES
    tile = pl.BlockSpec((1, ts, wid), lambda b, j, i: (b, i, j))
    par = pl.BlockSpec((1, wid), lambda b, j, i: (0, j))
    row = lambda t: t.reshape(1, d)
    return pl.pallas_call(
        functools.partial(_wkv_kernel, groups),
        grid=(bsz, d // wid, seq // ts),
        in_specs=[tile] * 6 + [par] * 5,
        out_specs=tile,
        out_shape=jax.ShapeDtypeStruct((bsz, seq, d), BF16),
        scratch_shapes=[pltpu.VMEM((groups, LANES, LANES), F32)],
        compiler_params=_cparams("parallel", "parallel", "arbitrary"),
        name="wkv7",
    )(r, k, v, ld, a, g, row(k_k), row(k_a), row(r_k), row(ln_w), row(ln_b))


def _proj_res_kernel(x_ref, y_ref, w_ref, gate_ref, o_ref):
    o_ref[0] = x_ref[0] + gate_ref[0] * _dot(y_ref[0], w_ref[...])


def _proj_res(x, y, w, gate, ts):
    bsz, seq, d = x.shape
    tile = pl.BlockSpec((1, ts, d), lambda b, i: (b, i, 0))
    return pl.pallas_call(
        _proj_res_kernel,
        grid=(bsz, seq // ts),
        in_specs=[tile, pl.BlockSpec((1, ts, y.shape[2]), lambda b, i: (b, i, 0)),
                  pl.BlockSpec(w.shape, lambda b, i: (0, 0)), pl.BlockSpec((1, 1, d), lambda b, i: (b, 0, 0))],
        out_specs=tile,
        out_shape=jax.ShapeDtypeStruct(x.shape, F32),
        compiler_params=_cparams("parallel", "parallel"),
        name="proj_res",
    )(x, y, w, gate)


FFN_HALO = 16


def _ffn_kernel(x_ref, xh_ref, nw_ref, sh_ref, sc_ref, gate_ref, wg_ref, wv_ref, cwg_ref, cwv_ref, cbg_ref, cbv_ref,
                wd_ref, o_ref, h_buf, u_buf, acc):
    i, f = pl.program_id(1), pl.program_id(2)
    ts = x_ref.shape[1]

    @pl.when(f == 0)
    def _():
        nw, sh, sc = nw_ref[...], sh_ref[0], sc_ref[0]
        halo = _rms(xh_ref[0], nw) * (1.0 + sc) + sh
        h_buf[0:FFN_HALO, :] = jnp.where(i > 0, halo, 0.0).astype(BF16)
        h_buf[FFN_HALO:, :] = (_rms(x_ref[0], nw) * (1.0 + sc) + sh).astype(BF16)
        acc[...] = jnp.zeros_like(acc)

    def conv(w_ref, cw_ref, cb_ref):
        u_buf[...] = _dot(h_buf[...], w_ref[...])
        return (cb_ref[...] + cw_ref[0:1, :] * u_buf[FFN_HALO - 2:FFN_HALO - 2 + ts, :]
                + cw_ref[1:2, :] * u_buf[FFN_HALO - 1:FFN_HALO - 1 + ts, :]
                + cw_ref[2:3, :] * u_buf[FFN_HALO:, :])

    gate = conv(wg_ref, cwg_ref, cbg_ref)
    val = conv(wv_ref, cwv_ref, cbv_ref)
    act = (gate * _sigmoid(gate) * val).astype(BF16)
    acc[...] += _dot(act, wd_ref[...])

    @pl.when(f == pl.num_programs(2) - 1)
    def _():
        o_ref[0] = x_ref[0] + gate_ref[0] * acc[...]


def _ffn(x, nw, sh, sc, gate, w_up, conv_w, conv_b, w_down, ts):
    bsz, seq, d = x.shape
    ff = w_down.shape[0]
    tf = ff // 2
    assert tf % LANES == 0
    nf = ff // tf
    tile = pl.BlockSpec((1, ts, d), lambda b, i, f: (b, i, 0))
    halo = pl.BlockSpec((1, FFN_HALO, d), lambda b, i, f: (b, jnp.maximum(i * (ts // FFN_HALO) - 1, 0), 0))
    per_b = pl.BlockSpec((1, 1, d), lambda b, i, f: (b, 0, 0))
    gcol = lambda rws: pl.BlockSpec((rws, tf), lambda b, i, f: (0, f))
    vcol = lambda rws: pl.BlockSpec((rws, tf), lambda b, i, f: (0, f + nf))
    conv_b = conv_b.reshape(1, -1)
    return pl.pallas_call(
        _ffn_kernel,
        grid=(bsz, seq // ts, nf),
        in_specs=[tile, halo, pl.BlockSpec((1, d), lambda b, i, f: (0, 0)), per_b, per_b, per_b,
                  gcol(d), vcol(d), gcol(3), vcol(3), gcol(1), vcol(1),
                  pl.BlockSpec((tf, d), lambda b, i, f: (f, 0))],
        out_specs=tile,
        out_shape=jax.ShapeDtypeStruct(x.shape, F32),
        scratch_shapes=[pltpu.VMEM((ts + FFN_HALO, d), BF16), pltpu.VMEM((ts + FFN_HALO, tf), F32),
                        pltpu.VMEM((ts, d), F32)],
        compiler_params=_cparams("parallel", "parallel", "arbitrary"),
        name="conv_glu_ffn",
    )(x, x, nw.reshape(1, d), sh, sc, gate, w_up, w_up, conv_w, conv_w, conv_b, conv_b, w_down)


def _rope_tables(seq):
    half = ROT_DIM // 2
    pos = jnp.arange(seq, dtype=F32)
    inv = ROPE_THETA ** (-jnp.arange(0, ROT_DIM, 2, dtype=F32) / ROT_DIM)
    ang = pos[:, None] * inv[None, :]
    cos, sin = jnp.cos(ang), jnp.sin(ang)
    ones = jnp.ones((seq, HEAD - ROT_DIM), F32)
    zeros = jnp.zeros((seq, HEAD - half), F32)
    cos_t = jnp.concatenate([cos, cos, ones], axis=1)
    sin_lo = jnp.concatenate([-sin, zeros], axis=1)
    sin_hi = jnp.concatenate([jnp.zeros((seq, half), F32), sin, zeros[:, half:]], axis=1)
    rep = lambda t: jnp.concatenate([t] * (LANES // HEAD), axis=1)
    return rep(cos_t), rep(sin_lo), rep(sin_hi)


def _rope(t, cos_t, sin_lo, sin_hi):
    half = ROT_DIM // 2
    outs = []
    for j in range(t.shape[1] // LANES):
        tj = t[:, j * LANES:(j + 1) * LANES]
        outs.append(tj * cos_t + pltpu.roll(tj, LANES - half, axis=1) * sin_lo + pltpu.roll(tj, half, axis=1) * sin_hi)
    return jnp.concatenate(outs, axis=1)


def _q_proj_kernel(x_ref, nw_ref, sh_ref, sc_ref, w_ref, cos_ref, sl_ref, sh2_ref, q_out):
    h = (_rms(x_ref[0], nw_ref[...]) * (1.0 + sc_ref[0]) + sh_ref[0]).astype(BF16)
    q = _rope(_dot(h, w_ref[...]), cos_ref[...], sl_ref[...], sh2_ref[...])
    q_out[0] = (q * (HEAD ** -0.5)).astype(q_out.dtype)


def _kv_proj_kernel(x_ref, nw_ref, w_ref, cos_ref, sl_ref, sh2_ref, k_out, v_out):
    h = _rms(x_ref[0], nw_ref[...]).astype(BF16)
    kv = _dot(h, w_ref[...])
    nk = k_out.shape[2]
    k_out[0] = _rope(kv[:, :nk], cos_ref[...], sl_ref[...], sh2_ref[...]).astype(k_out.dtype)
    v_out[0] = kv[:, nk:].astype(v_out.dtype)


def _q_proj(x, nw, sh, sc, w, tables, ts):
    bsz, seq, d = x.shape
    tile = pl.BlockSpec((1, ts, d), lambda b, i: (b, i, 0))
    per_b = pl.BlockSpec((1, 1, d), lambda b, i: (b, 0, 0))
    tab = pl.BlockSpec((ts, LANES), lambda b, i: (i, 0))
    return pl.pallas_call(
        _q_proj_kernel,
        grid=(bsz, seq // ts),
        in_specs=[tile, pl.BlockSpec((1, d), lambda b, i: (0, 0)), per_b, per_b,
                  pl.BlockSpec(w.shape, lambda b, i: (0, 0)), tab, tab, tab],
        out_specs=pl.BlockSpec((1, ts, w.shape[1]), lambda b, i: (b, i, 0)),
        out_shape=jax.ShapeDtypeStruct((bsz, seq, w.shape[1]), BF16),
        compiler_params=_cparams("parallel", "parallel"),
        name="q_proj",
    )(x, nw.reshape(1, d), sh, sc, w, *tables)


def _kv_proj(x, nw, w, qk_width, tables, ts):
    bsz, seq, d = x.shape
    v_width = w.shape[1] - qk_width
    tile = pl.BlockSpec((1, ts, d), lambda b, i: (b, i, 0))
    tab = pl.BlockSpec((ts, LANES), lambda b, i: (i, 0))
    return pl.pallas_call(
        _kv_proj_kernel,
        grid=(bsz, seq // ts),
        in_specs=[tile, pl.BlockSpec((1, d), lambda b, i: (0, 0)), pl.BlockSpec(w.shape, lambda b, i: (0, 0)),
                  tab, tab, tab],
        out_specs=[pl.BlockSpec((1, ts, qk_width), lambda b, i: (b, i, 0)),
                   pl.BlockSpec((1, ts, v_width), lambda b, i: (b, i, 0))],
        out_shape=[jax.ShapeDtypeStruct((bsz, seq, qk_width), BF16),
                   jax.ShapeDtypeStruct((bsz, seq, v_width), BF16)],
        compiler_params=_cparams("parallel", "parallel"),
        name="kv_proj",
    )(x, nw.reshape(1, d), w, *tables)


MASKED = -1e30


def _attn_kernel(lam_init, q_ref, k_ref, v_ref, lam_ref, sub_ref, o_ref):
    qi = pl.program_id(2)
    tq = q_ref.shape[1]
    q = q_ref[0]
    lane = lax.broadcasted_iota(jnp.int32, q.shape, 1)
    zero = jnp.zeros_like(q)
    qq = jnp.concatenate([jnp.where(lane < HEAD, q, zero), jnp.where(lane >= HEAD, q, zero)], axis=0)

    def step(j, carry, diagonal):
        m, l, acc = carry
        tok = pl.ds(pl.multiple_of(j * tq, tq), tq)
        s = _dot_nt(qq, k_ref[0, tok, :])
        if diagonal:
            qpos = lax.broadcasted_iota(jnp.int32, s.shape, 0) % tq
            kpos = lax.broadcasted_iota(jnp.int32, s.shape, 1)
            s = jnp.where(kpos <= qpos, s, MASKED)
        m_new = jnp.maximum(m, jnp.max(s, axis=1, keepdims=True))
        p = jnp.exp(s - m_new)
        alpha = jnp.exp(m - m_new)
        l = alpha * l + jnp.sum(p, axis=1, keepdims=True)
        acc = alpha * acc + _dot(p.astype(BF16), v_ref[0, tok, :])
        return m_new, l, acc

    init = (jnp.full((2 * tq, 1), MASKED, F32), jnp.zeros((2 * tq, 1), F32), jnp.zeros((2 * tq, 2 * HEAD), F32))
    carry = lax.fori_loop(0, qi, lambda j, c: step(j, c, False), init)
    _, l, acc = step(qi, carry, True)
    o = acc / l
    lv = lam_ref[...]
    lam = (jnp.exp(jnp.sum(lv[0:1] * lv[1:2], axis=1, keepdims=True))
           - jnp.exp(jnp.sum(lv[2:3] * lv[3:4], axis=1, keepdims=True)) + lam_init)
    dlt = o[:tq] - lam * o[tq:]
    o_ref[0] = (_rms(dlt, sub_ref[...]) * (1.0 - lam_init)).astype(o_ref.dtype)


def _diff_attn(q, k, v, lam_vecs, subln, lam_init, tq):
    bsz, seq, d = q.shape
    wid = 2 * HEAD
    return pl.pallas_call(
        functools.partial(_attn_kernel, lam_init),
        grid=(bsz, d // wid, seq // tq),
        in_specs=[pl.BlockSpec((1, tq, wid), lambda b, h, i: (b, i, h)),
                  pl.BlockSpec((1, seq, wid), lambda b, h, i: (b, 0, h)),
                  pl.BlockSpec((1, seq, wid), lambda b, h, i: (b, 0, h)),
                  pl.BlockSpec(lam_vecs.shape, lambda b, h, i: (0, 0)),
                  pl.BlockSpec((1, wid), lambda b, h, i: (0, 0))],
        out_specs=pl.BlockSpec((1, tq, wid), lambda b, h, i: (b, i, h)),
        out_shape=jax.ShapeDtypeStruct((bsz, seq, d), BF16),
        compiler_params=_cparams("parallel", "parallel", "arbitrary"),
        name="diff_attn",
    )(q, k, v, lam_vecs, subln.reshape(1, wid))


def _final_norm_kernel(x_ref, nw_ref, o_ref):
    o_ref[0] = _rms(x_ref[0], nw_ref[...])


def _final_norm(x, nw, ts):
    bsz, seq, d = x.shape
    tile = pl.BlockSpec((1, ts, d), lambda b, i: (b, i, 0))
    return pl.pallas_call(
        _final_norm_kernel,
        grid=(bsz, seq // ts),
        in_specs=[tile, pl.BlockSpec((1, d), lambda b, i: (0, 0))],
        out_specs=tile,
        out_shape=jax.ShapeDtypeStruct(x.shape, F32),
        compiler_params=_cparams("parallel", "parallel"),
        name="final_norm",
    )(x, nw.reshape(1, d))


def kernel(x, c, ada_w, ada_b, norm1, norm2, final_norm, a_mu, a_w_rkv, a_w0, a_w1, a_w2, a_a0, a_a1, a_a2, a_v0, a_v1, a_v2, a_g1, a_g2, a_k_k, a_k_a, a_r_k, a_ln_w, a_ln_b, a_w_o, kv_norm, w_kv, b_w_q, b_lam, b_subln, b_w_o, ffn_w_up, ffn_conv_w, ffn_conv_b, ffn_w_down):
    bsz, seq, d = x.shape
    depth = ada_w.shape[0]
    n_a = a_mu.shape[0]
    qk_width = b_w_q.shape[2]
    t = _tiles(seq)
    bf = lambda w: w.astype(BF16)
    tables = _rope_tables(seq)

    mod = _ada_mod(c, ada_w, ada_b)
    v_first = None
    k_sh = v_sh = None
    for l in range(depth):
        sh1, sc1, g1, sh2, sc2, g2 = (mod[l, :, j * d:(j + 1) * d].reshape(bsz, 1, d) for j in range(6))
        if l < n_a:
            vres = None if l == 0 else (a_v0[l - 1], bf(a_v1[l - 1]), bf(a_v2[l - 1]), v_first)
            r, k, v, ld, a, g = _rwkv_proj(
                x, norm1[l], sh1, sc1, a_mu[l], bf(a_w_rkv[l]), a_w0[l], bf(a_w1[l]), bf(a_w2[l]), a_a0[l],
                bf(a_a1[l]), bf(a_a2[l]), bf(a_g1[l]), bf(a_g2[l]), vres, t["proj"])
            if l == 0:
                v_first = v
            y = _wkv(r, k, v, ld, a, g, a_k_k[l], a_k_a[l], a_r_k[l], a_ln_w[l], a_ln_b[l], t["wkv"])
            w_o = a_w_o[l]
        else:
            j = l - n_a
            if j == 0:
                k_sh, v_sh = _kv_proj(x, kv_norm, bf(w_kv), qk_width, tables, t["qkv"])
            lam_init = 0.8 - 0.6 * math.exp(-0.3 * l)
            q = _q_proj(x, norm1[l], sh1, sc1, bf(b_w_q[j]), tables, t["qkv"])
            y = _diff_attn(q, k_sh, v_sh, b_lam[j], b_subln[j], lam_init, t["attn"])
            w_o = b_w_o[j]
        x = _proj_res(x, y, bf(w_o), g1, t["res"])
        x = _ffn(x, norm2[l], sh2, sc2, g2, bf(ffn_w_up[l]), ffn_conv_w[l], ffn_conv_b[l], bf(ffn_w_down[l]),
                 t["ffn"])
    return _final_norm(x, final_norm, t["fin"])
```

```python
import functools
import math

import jax
import jax.numpy as jnp
from jax import lax
from jax.experimental import pallas as pl
from jax.experimental.pallas import tpu as pltpu

F32 = jnp.float32
BF16 = jnp.bfloat16
HIGHEST = lax.Precision.HIGHEST

NORM_EPS = 1e-6
GN_EPS = 64e-5
ROPE_THETA = 500000.0
HEAD = 64
ROT_DIM = HEAD // 4
LANES = 128
WKV_CHUNK = 64
WKV_HEADS_PER_GROUP = LANES // HEAD
VMEM_LIMIT = 56 * 1024 * 1024


def _tiles(seq):
    def fit(t):
        t = min(t, seq)
        assert seq % t == 0
        return t
    return dict(proj=fit(256), wkv=fit(256), res=fit(512), ffn=fit(512), qkv=fit(512), attn=fit(512),
                fin=fit(512))


def _cparams(*sem):
    return pltpu.CompilerParams(dimension_semantics=sem, vmem_limit_bytes=VMEM_LIMIT)


def _dot(a, b, **kw):
    return jnp.dot(a, b, preferred_element_type=F32, **kw)


def _dot_nt(a, b):
    return lax.dot_general(a, b, (((1,), (1,)), ((), ())), preferred_element_type=F32)


def _dot_tn(a, b):
    return lax.dot_general(a, b, (((0,), (0,)), ((), ())), preferred_element_type=F32)


def _rms(x, g):
    return x * lax.rsqrt(jnp.mean(x * x, axis=-1, keepdims=True) + NORM_EPS) * g


def _sigmoid(x):
    return 1.0 / (1.0 + jnp.exp(-x))


def _softplus(x):
    return jnp.maximum(x, 0.0) + jnp.log1p(jnp.exp(-jnp.abs(x)))


def _mod_kernel(c_ref, w_ref, b_ref, o_ref):
    c = c_ref[...]
    o_ref[0] = _dot(c * _sigmoid(c), w_ref[0], precision=HIGHEST) + b_ref[0]


def _ada_mod(c, ada_w, ada_b):
    depth, d, n = ada_w.shape
    bsz = c.shape[0]
    tn = d
    return pl.pallas_call(
        _mod_kernel,
        grid=(depth, n // tn),
        in_specs=[pl.BlockSpec((bsz, d), lambda l, j: (0, 0)),
                  pl.BlockSpec((1, d, tn), lambda l, j: (l, 0, j)),
                  pl.BlockSpec((1, 1, tn), lambda l, j: (l, 0, j))],
        out_specs=pl.BlockSpec((1, bsz, tn), lambda l, j: (l, 0, j)),
        out_shape=jax.ShapeDtypeStruct((depth, bsz, n), F32),
        compiler_params=_cparams("parallel", "parallel"),
        name="ada_mod",
    )(c, ada_w, ada_b.reshape(depth, 1, n))


def _rwkv_proj_kernel(has_vres, x_ref, xh_ref, nw_ref, sh_ref, sc_ref, mu_ref, wrkv_ref,
                      w0_ref, w1_ref, w2_ref, a0_ref, a1_ref, a2_ref, g1_ref, g2_ref, *rest):
    if has_vres:
        v0_ref, v1_ref, v2_ref, vf_ref = rest[:4]
        rest = rest[4:]
    r_out, k_out, v_out, ld_out, a_out, g_out = rest
    i = pl.program_id(1)
    nw, sh, sc = nw_ref[...], sh_ref[0], sc_ref[0]

    def prep(xv):
        return _rms(xv, nw) * (1.0 + sc) + sh

    h = prep(x_ref[0])
    halo = prep(xh_ref[0])
    prev_last = jnp.where(i > 0, halo[7:8, :], 0.0)
    row = lax.broadcasted_iota(jnp.int32, h.shape, 0)
    hprev = jnp.where(row == 0, prev_last, pltpu.roll(h, 1, axis=0))
    xx = hprev - h

    def mix(j):
        return (h + xx * mu_ref[j:j + 1, :]).astype(BF16)

    xr, xw, xk, xv, xa, xg = (mix(j) for j in range(6))
    r_out[0] = _dot(xr, wrkv_ref[0])
    k_out[0] = _dot(xk, wrkv_ref[1])
    v = _dot(xv, wrkv_ref[2])
    if has_vres:
        gate = _sigmoid(v0_ref[...] + _dot(_dot(xv, v1_ref[...]).astype(BF16), v2_ref[...]))
        v = v + (vf_ref[0] - v) * gate
    v_out[0] = v
    wl = w0_ref[...] + _dot(jnp.tanh(_dot(xw, w1_ref[...])).astype(BF16), w2_ref[...])
    ld_out[0] = -jnp.exp(-_softplus(-wl) - 0.5)
    a_out[0] = _sigmoid(a0_ref[...] + _dot(_dot(xa, a1_ref[...]).astype(BF16), a2_ref[...]))
    g_out[0] = _dot(_sigmoid(_dot(xg, g1_ref[...])).astype(BF16), g2_ref[...]).astype(g_out.dtype)


def _rwkv_proj(x, nw, sh, sc, mu, wrkv, w0, w1, w2, a0, a1, a2, g1, g2, vres, ts):
    bsz, seq, d = x.shape
    has_vres = vres is not None
    row = lambda a: a.reshape(1, -1)
    full = lambda a: pl.BlockSpec(a.shape, lambda b, i: (0,) * a.ndim)
    tile = pl.BlockSpec((1, ts, d), lambda b, i: (b, i, 0))
    halo = pl.BlockSpec((1, 8, d), lambda b, i: (b, jnp.maximum(i * (ts // 8) - 1, 0), 0))
    per_b = pl.BlockSpec((1, 1, d), lambda b, i: (b, 0, 0))
    args = [x, x, row(nw), sh, sc, mu, wrkv, row(w0), w1, w2, row(a0), a1, a2, g1, g2]
    specs = [tile, halo, full(row(nw)), per_b, per_b, full(mu), full(wrkv), full(row(w0)), full(w1),
             full(w2), full(row(a0)), full(a1), full(a2), full(g1), full(g2)]
    if has_vres:
        v0, v1, v2, vf = vres
        args += [row(v0), v1, v2, vf]
        specs += [full(row(v0)), full(v1), full(v2), tile]
    out = jax.ShapeDtypeStruct((bsz, seq, d), F32)
    return pl.pallas_call(
        functools.partial(_rwkv_proj_kernel, has_vres),
        grid=(bsz, seq // ts),
        in_specs=specs,
        out_specs=[tile] * 6,
        out_shape=[out] * 5 + [jax.ShapeDtypeStruct((bsz, seq, d), BF16)],
        compiler_params=_cparams("parallel", "arbitrary"),
        name="rwkv_proj",
    )(*args)


def _wkv_consts(c, w):
    heads = w // HEAD
    rows = heads * c
    ti = lax.broadcasted_iota(jnp.int32, (c, 3 * c), 0)
    tj = lax.broadcasted_iota(jnp.int32, (c, 3 * c), 1) % c
    ri = lax.broadcasted_iota(jnp.int32, (rows, rows), 0)
    rj = lax.broadcasted_iota(jnp.int32, (rows, rows), 1)
    same = (ri // c) == (rj // c)
    incl = same & ((rj % c) <= (ri % c))
    return dict(
        tril3=(ti >= tj).astype(BF16),
        lane_head=lax.broadcasted_iota(jnp.int32, (c, w), 1) // HEAD,
        strict=same & ((rj % c) < (ri % c)),
        incl2=jnp.concatenate([incl, incl], axis=1))


def _wkv_chunk(inputs, params, states, consts):
    c, w = inputs[0][0].shape
    heads = w // HEAD
    rows = heads * c
    lane_head, strict, incl2 = consts["lane_head"], consts["strict"], consts["incl2"]

    def stack(x):
        return jnp.concatenate([jnp.where(lane_head == hd, x, 0.0) for hd in range(heads)], axis=0)

    def segsum(x):
        out = jnp.sum(jnp.where(lane_head == 0, x, 0.0), axis=1, keepdims=True)
        for hd in range(1, heads):
            out = jnp.where(lane_head == hd, jnp.sum(jnp.where(lane_head == hd, x, 0.0), axis=1, keepdims=True), out)
        return out

    def cumulative(ld):
        ld_hi = ld.astype(BF16)
        rem = ld - ld_hi.astype(F32)
        ld_mid = rem.astype(BF16)
        ld_lo = (rem - ld_mid.astype(F32)).astype(BF16)
        return _dot(consts["tril3"], jnp.concatenate([ld_hi, ld_mid, ld_lo], axis=0))

    def prepare(inp, par, cum):
        r, k, v, ld, a, _ = inp
        k_k, k_a = par[0], par[1]
        last = cum[c - 1:c, :]
        g_in, g_ex, g_inv, g_end = jnp.exp(cum), jnp.exp(cum - ld), jnp.exp(-cum), jnp.exp(last - cum)
        kk = k * k_k
        kk = kk / jnp.maximum(jnp.sqrt(segsum(kk * kk)), 1e-12)
        k2 = k * (1.0 + (a - 1.0) * k_a)
        b_in = kk * a
        at_s = stack(-kk * g_ex)
        rt_s = stack(r * g_in).astype(BF16)
        return dict(
            at_s=at_s, rt_s=rt_s, k2=k2, decay=jnp.exp(last), vs=stack(v).astype(BF16),
            lhs=jnp.concatenate([at_s.astype(BF16), rt_s], axis=0),
            rhs=jnp.concatenate([stack(b_in * g_inv), stack(k2 * g_inv)], axis=0).astype(BF16),
            bk_end=jnp.concatenate([stack(b_in * g_end), stack(k2 * g_end)], axis=0).astype(BF16))

    cums = [cumulative(inp[3]) for inp in inputs]
    pre = [prepare(inp, par, cum) for inp, par, cum in zip(inputs, params, cums)]
    aas = [_dot_nt(p["lhs"], p["rhs"]) for p in pre]
    a_aks = [jnp.where(strict, aa[:rows, rows:], 0.0).astype(BF16) for aa in aas]
    a_rs = [jnp.where(incl2, aa[rows:, :], 0.0).astype(BF16) for aa in aas]

    n_pows = [jnp.where(strict, aa[:rows, :rows], 0.0).astype(BF16) for aa in aas]
    xs = [jnp.concatenate([p["at_s"], _dot(a_ak, p["vs"])], axis=1) for p, a_ak in zip(pre, a_aks)]
    xs = [x + _dot(n, x.astype(BF16)) for x, n in zip(xs, n_pows)]
    for _ in range(int(math.log2(c)) - 1):
        n_pows = [_dot(n, n).astype(BF16) for n in n_pows]
        xs = [x + _dot(n, x.astype(BF16)) for x, n in zip(xs, n_pows)]

    pss = [_dot_nt(jnp.concatenate([x[:, :w].astype(BF16), p["rt_s"]], axis=0), st.astype(BF16))
           for x, p, st in zip(xs, pre, states)]
    uvs = [jnp.concatenate([(ps[:rows] + x[:, w:]).astype(BF16), p["vs"]], axis=0)
           for ps, x, p in zip(pss, xs, pre)]
    new_states = [st * p["decay"] + _dot_tn(uv, p["bk_end"]) for st, p, uv in zip(states, pre, uvs)]
    y_sts = [ps[rows:] + _dot(a_r, uv) for ps, a_r, uv in zip(pss, a_rs, uvs)]

    def finish(inp, par, p, y_st):
        r, _, v, _, _, g = inp
        r_k, ln_w, ln_b = par[2], par[3], par[4]
        y = y_st[:c]
        for hd in range(1, heads):
            y = y + y_st[hd * c:(hd + 1) * c]
        mean = segsum(y) * (1.0 / HEAD)
        dlt = y - mean
        var = segsum(dlt * dlt) * (1.0 / HEAD)
        out = dlt * lax.rsqrt(var + GN_EPS) * ln_w + ln_b + segsum(r * p["k2"] * r_k) * v
        return out * g

    outs = [finish(inp, par, p, y_st) for inp, par, p, y_st in zip(inputs, params, pre, y_sts)]
    return outs, new_states


def _wkv_kernel(groups, r_ref, k_ref, v_ref, ld_ref, a_ref, g_ref, kk_ref, ka_ref, rk_ref, lnw_ref, lnb_ref,
                o_ref, s_ref):
    @pl.when(pl.program_id(2) == 0)
    def _():
        s_ref[...] = jnp.zeros_like(s_ref)

    n_chunks = r_ref.shape[1] // WKV_CHUNK
    consts = _wkv_consts(WKV_CHUNK, LANES)
    lanes = [pl.ds(gi * LANES, LANES) for gi in range(groups)]

    def body(ci, carry):
        tok = pl.ds(pl.multiple_of(ci * WKV_CHUNK, WKV_CHUNK), WKV_CHUNK)
        inputs = [(r_ref[0, tok, ln], k_ref[0, tok, ln], v_ref[0, tok, ln], ld_ref[0, tok, ln], a_ref[0, tok, ln],
                   g_ref[0, tok, ln].astype(F32)) for ln in lanes]
        params = [(kk_ref[:, ln], ka_ref[:, ln], rk_ref[:, ln], lnw_ref[:, ln], lnb_ref[:, ln]) for ln in lanes]
        outs, new_states = _wkv_chunk(inputs, params, [s_ref[gi] for gi in range(groups)], consts)
        for gi, ln in enumerate(lanes):
            s_ref[gi] = new_states[gi]
            o_ref[0, tok, ln] = outs[gi].astype(o_ref.dtype)
        return carry

    lax.fori_loop(0, n_chunks, body, 0)


def _wkv(r, k, v, ld, a, g, k_k, k_a, r_k, ln_w, ln_b, ts, groups=8):
    bsz, seq, d = r.shape
    wid = groups * LANES
    tile = pl.BlockSpec((1, ts, wid), lambda b, j, i: (b, i, j))
    par = pl.BlockSpec((1, wid), lambda b, j, i: (0, j))
    row = lambda t: t.reshape(1, d)
    return pl.pallas_call(
        functools.partial(_wkv_kernel, groups),
        grid=(bsz, d // wid, seq // ts),
        in_specs=[tile] * 6 + [par] * 5,
        out_specs=tile,
        out_shape=jax.ShapeDtypeStruct((bsz, seq, d), BF16),
        scratch_shapes=[pltpu.VMEM((groups, LANES, LANES), F32)],
        compiler_params=_cparams("parallel", "parallel", "arbitrary"),
        name="wkv7",
    )(r, k, v, ld, a, g, row(k_k), row(k_a), row(r_k), row(ln_w), row(ln_b))


def _proj_res_kernel(x_ref, y_ref, w_ref, gate_ref, o_ref):
    o_ref[0] = x_ref[0] + gate_ref[0] * _dot(y_ref[0], w_ref[...])


def _proj_res(x, y, w, gate, ts):
    bsz, seq, d = x.shape
    tile = pl.BlockSpec((1, ts, d), lambda b, i: (b, i, 0))
    return pl.pallas_call(
        _proj_res_kernel,
        grid=(bsz, seq // ts),
        in_specs=[tile, pl.BlockSpec((1, ts, y.shape[2]), lambda b, i: (b, i, 0)),
                  pl.BlockSpec(w.shape, lambda b, i: (0, 0)), pl.BlockSpec((1, 1, d), lambda b, i: (b, 0, 0))],
        out_specs=tile,
        out_shape=jax.ShapeDtypeStruct(x.shape, F32),
        compiler_params=_cparams("parallel", "parallel"),
        name="proj_res",
    )(x, y, w, gate)


FFN_HALO = 16


def _ffn_kernel(x_ref, xh_ref, nw_ref, sh_ref, sc_ref, gate_ref, wg_ref, wv_ref, cwg_ref, cwv_ref, cbg_ref, cbv_ref,
                wd_ref, o_ref, h_buf, u_buf, acc):
    i, f = pl.program_id(1), pl.program_id(2)
    ts = x_ref.shape[1]

    @pl.when(f == 0)
    def _():
        nw, sh, sc = nw_ref[...], sh_ref[0], sc_ref[0]
        halo = _rms(xh_ref[0], nw) * (1.0 + sc) + sh
        h_buf[0:FFN_HALO, :] = jnp.where(i > 0, halo, 0.0).astype(BF16)
        h_buf[FFN_HALO:, :] = (_rms(x_ref[0], nw) * (1.0 + sc) + sh).astype(BF16)
        acc[...] = jnp.zeros_like(acc)

    def conv(w_ref, cw_ref, cb_ref):
        u_buf[...] = _dot(h_buf[...], w_ref[...])
        return (cb_ref[...] + cw_ref[0:1, :] * u_buf[FFN_HALO - 2:FFN_HALO - 2 + ts, :]
                + cw_ref[1:2, :] * u_buf[FFN_HALO - 1:FFN_HALO - 1 + ts, :]
                + cw_ref[2:3, :] * u_buf[FFN_HALO:, :])

    gate = conv(wg_ref, cwg_ref, cbg_ref)
    val = conv(wv_ref, cwv_ref, cbv_ref)
    act = (gate * _sigmoid(gate) * val).astype(BF16)
    acc[...] += _dot(act, wd_ref[...])

    @pl.when(f == pl.num_programs(2) - 1)
    def _():
        o_ref[0] = x_ref[0] + gate_ref[0] * acc[...]


def _ffn(x, nw, sh, sc, gate, w_up, conv_w, conv_b, w_down, ts):
    bsz, seq, d = x.shape
    ff = w_down.shape[0]
    tf = ff // 2
    assert tf % LANES == 0
    nf = ff // tf
    tile = pl.BlockSpec((1, ts, d), lambda b, i, f: (b, i, 0))
    halo = pl.BlockSpec((1, FFN_HALO, d), lambda b, i, f: (b, jnp.maximum(i * (ts // FFN_HALO) - 1, 0), 0))
    per_b = pl.BlockSpec((1, 1, d), lambda b, i, f: (b, 0, 0))
    gcol = lambda rws: pl.BlockSpec((rws, tf), lambda b, i, f: (0, f))
    vcol = lambda rws: pl.BlockSpec((rws, tf), lambda b, i, f: (0, f + nf))
    conv_b = conv_b.reshape(1, -1)
    return pl.pallas_call(
        _ffn_kernel,
        grid=(bsz, seq // ts, nf),
        in_specs=[tile, halo, pl.BlockSpec((1, d), lambda b, i, f: (0, 0)), per_b, per_b, per_b,
                  gcol(d), vcol(d), gcol(3), vcol(3), gcol(1), vcol(1),
                  pl.BlockSpec((tf, d), lambda b, i, f: (f, 0))],
        out_specs=tile,
        out_shape=jax.ShapeDtypeStruct(x.shape, F32),
        scratch_shapes=[pltpu.VMEM((ts + FFN_HALO, d), BF16), pltpu.VMEM((ts + FFN_HALO, tf), F32),
                        pltpu.VMEM((ts, d), F32)],
        compiler_params=_cparams("parallel", "parallel", "arbitrary"),
        name="conv_glu_ffn",
    )(x, x, nw.reshape(1, d), sh, sc, gate, w_up, w_up, conv_w, conv_w, conv_b, conv_b, w_down)


def _rope_tables(seq):
    half = ROT_DIM // 2
    pos = jnp.arange(seq, dtype=F32)
    inv = ROPE_THETA ** (-jnp.arange(0, ROT_DIM, 2, dtype=F32) / ROT_DIM)
    ang = pos[:, None] * inv[None, :]
    cos, sin = jnp.cos(ang), jnp.sin(ang)
    ones = jnp.ones((seq, HEAD - ROT_DIM), F32)
    zeros = jnp.zeros((seq, HEAD - half), F32)
    cos_t = jnp.concatenate([cos, cos, ones], axis=1)
    sin_lo = jnp.concatenate([-sin, zeros], axis=1)
    sin_hi = jnp.concatenate([jnp.zeros((seq, half), F32), sin, zeros[:, half:]], axis=1)
    rep = lambda t: jnp.concatenate([t] * (LANES // HEAD), axis=1)
    return rep(cos_t), rep(sin_lo), rep(sin_hi)


def _rope(t, cos_t, sin_lo, sin_hi):
    half = ROT_DIM // 2
    outs = []
    for j in range(t.shape[1] // LANES):
        tj = t[:, j * LANES:(j + 1) * LANES]
        outs.append(tj * cos_t + pltpu.roll(tj, LANES - half, axis=1) * sin_lo + pltpu.roll(tj, half, axis=1) * sin_hi)
    return jnp.concatenate(outs, axis=1)


def _q_proj_kernel(x_ref, nw_ref, sh_ref, sc_ref, w_ref, cos_ref, sl_ref, sh2_ref, q_out):
    h = (_rms(x_ref[0], nw_ref[...]) * (1.0 + sc_ref[0]) + sh_ref[0]).astype(BF16)
    q = _rope(_dot(h, w_ref[...]), cos_ref[...], sl_ref[...], sh2_ref[...])
    q_out[0] = (q * (HEAD ** -0.5)).astype(q_out.dtype)


def _kv_proj_kernel(x_ref, nw_ref, w_ref, cos_ref, sl_ref, sh2_ref, k_out, v_out):
    h = _rms(x_ref[0], nw_ref[...]).astype(BF16)
    kv = _dot(h, w_ref[...])
    nk = k_out.shape[2]
    k_out[0] = _rope(kv[:, :nk], cos_ref[...], sl_ref[...], sh2_ref[...]).astype(k_out.dtype)
    v_out[0] = kv[:, nk:].astype(v_out.dtype)


def _q_proj(x, nw, sh, sc, w, tables, ts):
    bsz, seq, d = x.shape
    tile = pl.BlockSpec((1, ts, d), lambda b, i: (b, i, 0))
    per_b = pl.BlockSpec((1, 1, d), lambda b, i: (b, 0, 0))
    tab = pl.BlockSpec((ts, LANES), lambda b, i: (i, 0))
    return pl.pallas_call(
        _q_proj_kernel,
        grid=(bsz, seq // ts),
        in_specs=[tile, pl.BlockSpec((1, d), lambda b, i: (0, 0)), per_b, per_b,
                  pl.BlockSpec(w.shape, lambda b, i: (0, 0)), tab, tab, tab],
        out_specs=pl.BlockSpec((1, ts, w.shape[1]), lambda b, i: (b, i, 0)),
        out_shape=jax.ShapeDtypeStruct((bsz, seq, w.shape[1]), BF16),
        compiler_params=_cparams("parallel", "parallel"),
        name="q_proj",
    )(x, nw.reshape(1, d), sh, sc, w, *tables)


def _kv_proj(x, nw, w, qk_width, tables, ts):
    bsz, seq, d = x.shape
    v_width = w.shape[1] - qk_width
    tile = pl.BlockSpec((1, ts, d), lambda b, i: (b, i, 0))
    tab = pl.BlockSpec((ts, LANES), lambda b, i: (i, 0))
    return pl.pallas_call(
        _kv_proj_kernel,
        grid=(bsz, seq // ts),
        in_specs=[tile, pl.BlockSpec((1, d), lambda b, i: (0, 0)), pl.BlockSpec(w.shape, lambda b, i: (0, 0)),
                  tab, tab, tab],
        out_specs=[pl.BlockSpec((1, ts, qk_width), lambda b, i: (b, i, 0)),
                   pl.BlockSpec((1, ts, v_width), lambda b, i: (b, i, 0))],
        out_shape=[jax.ShapeDtypeStruct((bsz, seq, qk_width), BF16),
                   jax.ShapeDtypeStruct((bsz, seq, v_width), BF16)],
        compiler_params=_cparams("parallel", "parallel"),
        name="kv_proj",
    )(x, nw.reshape(1, d), w, *tables)


MASKED = -1e30


def _attn_kernel(lam_init, q_ref, k_ref, v_ref, lam_ref, sub_ref, o_ref):
    qi = pl.program_id(2)
    tq = q_ref.shape[1]
    q = q_ref[0]
    lane = lax.broadcasted_iota(jnp.int32, q.shape, 1)
    zero = jnp.zeros_like(q)
    qq = jnp.concatenate([jnp.where(lane < HEAD, q, zero), jnp.where(lane >= HEAD, q, zero)], axis=0)

    def step(j, carry, diagonal):
        m, l, acc = carry
        tok = pl.ds(pl.multiple_of(j * tq, tq), tq)
        s = _dot_nt(qq, k_ref[0, tok, :])
        if diagonal:
            qpos = lax.broadcasted_iota(jnp.int32, s.shape, 0) % tq
            kpos = lax.broadcasted_iota(jnp.int32, s.shape, 1)
            s = jnp.where(kpos <= qpos, s, MASKED)
        m_new = jnp.maximum(m, jnp.max(s, axis=1, keepdims=True))
        p = jnp.exp(s - m_new)
        alpha = jnp.exp(m - m_new)
        l = alpha * l + jnp.sum(p, axis=1, keepdims=True)
        acc = alpha * acc + _dot(p.astype(BF16), v_ref[0, tok, :])
        return m_new, l, acc

    init = (jnp.full((2 * tq, 1), MASKED, F32), jnp.zeros((2 * tq, 1), F32), jnp.zeros((2 * tq, 2 * HEAD), F32))
    carry = lax.fori_loop(0, qi, lambda j, c: step(j, c, False), init)
    _, l, acc = step(qi, carry, True)
    o = acc / l
    lv = lam_ref[...]
    lam = (jnp.exp(jnp.sum(lv[0:1] * lv[1:2], axis=1, keepdims=True))
           - jnp.exp(jnp.sum(lv[2:3] * lv[3:4], axis=1, keepdims=True)) + lam_init)
    dlt = o[:tq] - lam * o[tq:]
    o_ref[0] = (_rms(dlt, sub_ref[...]) * (1.0 - lam_init)).astype(o_ref.dtype)


def _diff_attn(q, k, v, lam_vecs, subln, lam_init, tq):
    bsz, seq, d = q.shape
    wid = 2 * HEAD
    return pl.pallas_call(
        functools.partial(_attn_kernel, lam_init),
        grid=(bsz, d // wid, seq // tq),
        in_specs=[pl.BlockSpec((1, tq, wid), lambda b, h, i: (b, i, h)),
                  pl.BlockSpec((1, seq, wid), lambda b, h, i: (b, 0, h)),
                  pl.BlockSpec((1, seq, wid), lambda b, h, i: (b, 0, h)),
                  pl.BlockSpec(lam_vecs.shape, lambda b, h, i: (0, 0)),
                  pl.BlockSpec((1, wid), lambda b, h, i: (0, 0))],
        out_specs=pl.BlockSpec((1, tq, wid), lambda b, h, i: (b, i, h)),
        out_shape=jax.ShapeDtypeStruct((bsz, seq, d), BF16),
        compiler_params=_cparams("parallel", "parallel", "arbitrary"),
        name="diff_attn",
    )(q, k, v, lam_vecs, subln.reshape(1, wid))


def _final_norm_kernel(x_ref, nw_ref, o_ref):
    o_ref[0] = _rms(x_ref[0], nw_ref[...])


def _final_norm(x, nw, ts):
    bsz, seq, d = x.shape
    tile = pl.BlockSpec((1, ts, d), lambda b, i: (b, i, 0))
    return pl.pallas_call(
        _final_norm_kernel,
        grid=(bsz, seq // ts),
        in_specs=[tile, pl.BlockSpec((1, d), lambda b, i: (0, 0))],
        out_specs=tile,
        out_shape=jax.ShapeDtypeStruct(x.shape, F32),
        compiler_params=_cparams("parallel", "parallel"),
        name="final_norm",
    )(x, nw.reshape(1, d))


def kernel(x, c, ada_w, ada_b, norm1, norm2, final_norm, a_mu, a_w_rkv, a_w0, a_w1, a_w2, a_a0, a_a1, a_a2, a_v0, a_v1, a_v2, a_g1, a_g2, a_k_k, a_k_a, a_r_k, a_ln_w, a_ln_b, a_w_o, kv_norm, w_kv, b_w_q, b_lam, b_subln, b_w_o, ffn_w_up, ffn_conv_w, ffn_conv_b, ffn_w_down):
    bsz, seq, d = x.shape
    depth = ada_w.shape[0]
    n_a = a_mu.shape[0]
    qk_width = b_w_q.shape[2]
    t = _tiles(seq)
    bf = lambda w: w.astype(BF16)
    tables = _rope_tables(seq)

    mod = _ada_mod(c, ada_w, ada_b)
    v_first = None
    k_sh = v_sh = None
    for l in range(depth):
        sh1, sc1, g1, sh2, sc2, g2 = (mod[l, :, j * d:(j + 1) * d].reshape(bsz, 1, d) for j in range(6))
        if l < n_a:
            vres = None if l == 0 else (a_v0[l - 1], bf(a_v1[l - 1]), bf(a_v2[l - 1]), v_first)
            r, k, v, ld, a, g = _rwkv_proj(
                x, norm1[l], sh1, sc1, a_mu[l], bf(a_w_rkv[l]), a_w0[l], bf(a_w1[l]), bf(a_w2[l]), a_a0[l],
                bf(a_a1[l]), bf(a_a2[l]), bf(a_g1[l]), bf(a_g2[l]), vres, t["proj"])
            if l == 0:
                v_first = v
            y = _wkv(r, k, v, ld, a, g, a_k_k[l], a_k_a[l], a_r_k[l], a_ln_w[l], a_ln_b[l], t["wkv"])
            w_o = a_w_o[l]
        else:
            j = l - n_a
            if j == 0:
                k_sh, v_sh = _kv_proj(x, kv_norm, bf(w_kv), qk_width, tables, t["qkv"])
            lam_init = 0.8 - 0.6 * math.exp(-0.3 * l)
            q = _q_proj(x, norm1[l], sh1, sc1, bf(b_w_q[j]), tables, t["qkv"])
            y = _diff_attn(q, k_sh, v_sh, b_lam[j], b_subln[j], lam_init, t["attn"])
            w_o = b_w_o[j]
        x = _proj_res(x, y, bf(w_o), g1, t["res"])
        x = _ffn(x, norm2[l], sh2, sc2, g2, bf(ffn_w_up[l]), ffn_conv_w[l], ffn_conv_b[l], bf(ffn_w_down[l]),
                 t["ffn"])
    return _final_norm(x, final_norm, t["fin"])
```

```python
import functools
import math

import jax
import jax.numpy as jnp
from jax import lax
from jax.experimental import pallas as pl
from jax.experimental.pallas import tpu as pltpu

F32 = jnp.float32
BF16 = jnp.bfloat16
HIGHEST = lax.Precision.HIGHEST

NORM_EPS = 1e-6
GN_EPS = 64e-5
ROPE_THETA = 500000.0
LOG2_E = math.log2(math.e)
HEAD = 64
ROT_DIM = HEAD // 4
LANES = 128
WKV_CHUNK = 64
WKV_HEADS_PER_GROUP = LANES // HEAD
VMEM_LIMIT = 56 * 1024 * 1024


def _tiles(seq):
    def fit(t):
        t = min(t, seq)
        assert seq % t == 0
        return t
    return dict(proj=fit(512), wkv=fit(256), ffn=fit(512), qkv=fit(512), attn=fit(512))


def _cparams(*sem):
    return pltpu.CompilerParams(dimension_semantics=sem, vmem_limit_bytes=VMEM_LIMIT)


def _dot(a, b, **kw):
    return jnp.dot(a, b, preferred_element_type=F32, **kw)


def _dot_nt(a, b):
    return lax.dot_general(a, b, (((1,), (1,)), ((), ())), preferred_element_type=F32)


def _dot_tn(a, b):
    return lax.dot_general(a, b, (((0,), (0,)), ((), ())), preferred_element_type=F32)


def _rms(x, g):
    return x * lax.rsqrt(jnp.mean(x * x, axis=-1, keepdims=True) + NORM_EPS) * g


def _sigmoid(x):
    return 1.0 / (1.0 + jnp.exp(-x))


def _softplus(x):
    return jnp.maximum(x, 0.0) + jnp.log1p(jnp.exp(-jnp.abs(x)))


def _mod_kernel(c_ref, w_ref, b_ref, o_ref):
    c = c_ref[...]
    o_ref[0] = _dot(c * _sigmoid(c), w_ref[0], precision=HIGHEST) + b_ref[0]


def _ada_mod(c, ada_w, ada_b):
    depth, d, n = ada_w.shape
    bsz = c.shape[0]
    tn = d
    return pl.pallas_call(
        _mod_kernel,
        grid=(depth, n // tn),
        in_specs=[pl.BlockSpec((bsz, d), lambda l, j: (0, 0)),
                  pl.BlockSpec((1, d, tn), lambda l, j: (l, 0, j)),
                  pl.BlockSpec((1, 1, tn), lambda l, j: (l, 0, j))],
        out_specs=pl.BlockSpec((1, bsz, tn), lambda l, j: (l, 0, j)),
        out_shape=jax.ShapeDtypeStruct((depth, bsz, n), F32),
        compiler_params=_cparams("parallel", "parallel"),
        name="ada_mod",
    )(c, ada_w, ada_b.reshape(depth, 1, n))


def _rwkv_proj_kernel(has_vres, x_ref, xh_ref, nw_ref, sh_ref, sc_ref, mu_ref, wrkv_ref,
                      w0_ref, w1_ref, w2_ref, a0_ref, a1_ref, a2_ref, g1_ref, g2_ref, *rest):
    if has_vres:
        v0_ref, v1_ref, v2_ref, vf_ref = rest[:4]
        rest = rest[4:]
    r_out, k_out, v_out, ld_out, a_out, g_out = rest
    i = pl.program_id(1)
    nw, sh, sc = nw_ref[...], sh_ref[0], sc_ref[0]

    def prep(xv):
        return _rms(xv, nw) * (1.0 + sc) + sh

    h = prep(x_ref[0])
    halo = prep(xh_ref[0])
    prev_last = jnp.where(i > 0, halo[7:8, :], 0.0)
    row = lax.broadcasted_iota(jnp.int32, h.shape, 0)
    hprev = jnp.where(row == 0, prev_last, pltpu.roll(h, 1, axis=0))
    xx = hprev - h

    def mix(j):
        return (h + xx * mu_ref[j:j + 1, :]).astype(BF16)

    xr, xw, xk, xv, xa, xg = (mix(j) for j in range(6))
    r_out[0] = _dot(xr, wrkv_ref[0])
    k_out[0] = _dot(xk, wrkv_ref[1])
    v = _dot(xv, wrkv_ref[2])
    if has_vres:
        gate = _sigmoid(v0_ref[...] + _dot(_dot(xv, v1_ref[...]).astype(BF16), v2_ref[...]))
        v = v + (vf_ref[0] - v) * gate
    v_out[0] = v
    wl = w0_ref[...] + _dot(jnp.tanh(_dot(xw, w1_ref[...])).astype(BF16), w2_ref[...])
    ld_out[0] = -jnp.exp(-_softplus(-wl) - 0.5)
    a_out[0] = _sigmoid(a0_ref[...] + _dot(_dot(xa, a1_ref[...]).astype(BF16), a2_ref[...]))
    g_out[0] = _dot(_sigmoid(_dot(xg, g1_ref[...])).astype(BF16), g2_ref[...]).astype(g_out.dtype)


def _rwkv_proj(x, nw, sh, sc, mu, wrkv, w0, w1, w2, a0, a1, a2, g1, g2, vres, ts):
    bsz, seq, d = x.shape
    has_vres = vres is not None
    row = lambda a: a.reshape(1, -1)
    full = lambda a: pl.BlockSpec(a.shape, lambda b, i: (0,) * a.ndim, pipeline_mode=pl.Buffered(1))
    tile = pl.BlockSpec((1, ts, d), lambda b, i: (b, i, 0))
    halo = pl.BlockSpec((1, 8, d), lambda b, i: (b, jnp.maximum(i * (ts // 8) - 1, 0), 0))
    per_b = pl.BlockSpec((1, 1, d), lambda b, i: (b, 0, 0))
    args = [x, x, row(nw), sh, sc, mu, wrkv, row(w0), w1, w2, row(a0), a1, a2, g1, g2]
    specs = [tile, halo, full(row(nw)), per_b, per_b, full(mu), full(wrkv), full(row(w0)), full(w1),
             full(w2), full(row(a0)), full(a1), full(a2), full(g1), full(g2)]
    if has_vres:
        v0, v1, v2, vf = vres
        args += [row(v0), v1, v2, vf]
        specs += [full(row(v0)), full(v1), full(v2), tile]
    out = jax.ShapeDtypeStruct((bsz, seq, d), F32)
    return pl.pallas_call(
        functools.partial(_rwkv_proj_kernel, has_vres),
        grid=(bsz, seq // ts),
        in_specs=specs,
        out_specs=[tile] * 6,
        out_shape=[out] * 5 + [jax.ShapeDtypeStruct((bsz, seq, d), BF16)],
        compiler_params=_cparams("parallel", "arbitrary"),
        name="rwkv_proj",
    )(*args)


def _wkv_consts(c, w):
    heads = w // HEAD
    rows = heads * c
    ti = lax.broadcasted_iota(jnp.int32, (c, 3 * c), 0)
    tj = lax.broadcasted_iota(jnp.int32, (c, 3 * c), 1) % c
    ri = lax.broadcasted_iota(jnp.int32, (rows, rows), 0)
    rj = lax.broadcasted_iota(jnp.int32, (rows, rows), 1)
    same = (ri // c) == (rj // c)
    incl = same & ((rj % c) <= (ri % c))
    return dict(
        tril3=(ti >= tj).astype(BF16),
        lane_head=lax.broadcasted_iota(jnp.int32, (c, w), 1) // HEAD,
        strict=same & ((rj % c) < (ri % c)),
        incl2=jnp.concatenate([incl, incl], axis=1))


def _wkv_chunk(inputs, params, states, consts):
    c, w = inputs[0][0].shape
    heads = w // HEAD
    rows = heads * c
    lane_head, strict, incl2 = consts["lane_head"], consts["strict"], consts["incl2"]

    def stack(x):
        return jnp.concatenate([jnp.where(lane_head == hd, x, 0.0) for hd in range(heads)], axis=0)

    def segsum(x):
        out = jnp.sum(jnp.where(lane_head == 0, x, 0.0), axis=1, keepdims=True)
        for hd in range(1, heads):
            out = jnp.where(lane_head == hd, jnp.sum(jnp.where(lane_head == hd, x, 0.0), axis=1, keepdims=True), out)
        return out

    def cumulative(ld):
        ld_hi = ld.astype(BF16)
        rem = ld - ld_hi.astype(F32)
        ld_mid = rem.astype(BF16)
        ld_lo = (rem - ld_mid.astype(F32)).astype(BF16)
        return _dot(consts["tril3"], jnp.concatenate([ld_hi, ld_mid, ld_lo], axis=0))

    def prepare(inp, par, cum):
        r, k, v, ld, a, _ = inp
        k_k, k_a = par[0], par[1]
        last = cum[c - 1:c, :]
        g_in, g_ex, g_inv, g_end = jnp.exp(cum), jnp.exp(cum - ld), jnp.exp(-cum), jnp.exp(last - cum)
        kk = k * k_k
        kk = kk / jnp.maximum(jnp.sqrt(segsum(kk * kk)), 1e-12)
        k2 = k * (1.0 + (a - 1.0) * k_a)
        b_in = kk * a
        at_s = stack(-kk * g_ex)
        rt_s = stack(r * g_in).astype(BF16)
        return dict(
            at_s=at_s, rt_s=rt_s, k2=k2, decay=jnp.exp(last), vs=stack(v).astype(BF16),
            lhs=jnp.concatenate([at_s.astype(BF16), rt_s], axis=0),
            rhs=jnp.concatenate([stack(b_in * g_inv), stack(k2 * g_inv)], axis=0).astype(BF16),
            bk_end=jnp.concatenate([stack(b_in * g_end), stack(k2 * g_end)], axis=0).astype(BF16))

    cums = [cumulative(inp[3]) for inp in inputs]
    pre = [prepare(inp, par, cum) for inp, par, cum in zip(inputs, params, cums)]
    aas = [_dot_nt(p["lhs"], p["rhs"]) for p in pre]
    a_aks = [jnp.where(strict, aa[:rows, rows:], 0.0).astype(BF16) for aa in aas]
    a_rs = [jnp.where(incl2, aa[rows:, :], 0.0).astype(BF16) for aa in aas]

    n_pows = [jnp.where(strict, aa[:rows, :rows], 0.0).astype(BF16) for aa in aas]
    xs = [jnp.concatenate([p["at_s"], _dot(a_ak, p["vs"])], axis=1) for p, a_ak in zip(pre, a_aks)]
    xs = [x + _dot(n, x.astype(BF16)) for x, n in zip(xs, n_pows)]
    for _ in range(int(math.log2(c)) - 1):
        n_pows = [_dot(n, n).astype(BF16) for n in n_pows]
        xs = [x + _dot(n, x.astype(BF16)) for x, n in zip(xs, n_pows)]

    pss = [_dot_nt(jnp.concatenate([x[:, :w].astype(BF16), p["rt_s"]], axis=0), st.astype(BF16))
           for x, p, st in zip(xs, pre, states)]
    uvs = [jnp.concatenate([(ps[:rows] + x[:, w:]).astype(BF16), p["vs"]], axis=0)
           for ps, x, p in zip(pss, xs, pre)]
    new_states = [st * p["decay"] + _dot_tn(uv, p["bk_end"]) for st, p, uv in zip(states, pre, uvs)]
    y_sts = [ps[rows:] + _dot(a_r, uv) for ps, a_r, uv in zip(pss, a_rs, uvs)]

    def finish(inp, par, p, y_st):
        r, _, v, _, _, g = inp
        r_k, ln_w, ln_b = par[2], par[3], par[4]
        y = y_st[:c]
        for hd in range(1, heads):
            y = y + y_st[hd * c:(hd + 1) * c]
        mean = segsum(y) * (1.0 / HEAD)
        dlt = y - mean
        var = segsum(dlt * dlt) * (1.0 / HEAD)
        out = dlt * lax.rsqrt(var + GN_EPS) * ln_w + ln_b + segsum(r * p["k2"] * r_k) * v
        return out * g

    outs = [finish(inp, par, p, y_st) for inp, par, p, y_st in zip(inputs, params, pre, y_sts)]
    return outs, new_states


def _wkv_kernel(groups, r_ref, k_ref, v_ref, ld_ref, a_ref, g_ref, kk_ref, ka_ref, rk_ref, lnw_ref, lnb_ref,
                o_ref, s_ref):
    @pl.when(pl.program_id(2) == 0)
    def _():
        s_ref[...] = jnp.zeros_like(s_ref)

    n_chunks = r_ref.shape[1] // WKV_CHUNK
    consts = _wkv_consts(WKV_CHUNK, LANES)
    lanes = [pl.ds(gi * LANES, LANES) for gi in range(groups)]

    def body(ci, carry):
        tok = pl.ds(pl.multiple_of(ci * WKV_CHUNK, WKV_CHUNK), WKV_CHUNK)
        inputs = [(r_ref[0, tok, ln], k_ref[0, tok, ln], v_ref[0, tok, ln], ld_ref[0, tok, ln], a_ref[0, tok, ln],
                   g_ref[0, tok, ln].astype(F32)) for ln in lanes]
        params = [(kk_ref[:, ln], ka_ref[:, ln], rk_ref[:, ln], lnw_ref[:, ln], lnb_ref[:, ln]) for ln in lanes]
        outs, new_states = _wkv_chunk(inputs, params, [s_ref[gi] for gi in range(groups)], consts)
        for gi, ln in enumerate(lanes):
            s_ref[gi] = new_states[gi]
            o_ref[0, tok, ln] = outs[gi].astype(o_ref.dtype)
        return carry

    lax.fori_loop(0, n_chunks, body, 0)


def _wkv(r, k, v, ld, a, g, k_k, k_a, r_k, ln_w, ln_b, ts, groups=8):
    bsz, seq, d = r.shape
    wid = groups * LANES
    tile = pl.BlockSpec((1, ts, wid), lambda b, j, i: (b, i, j))
    par = pl.BlockSpec((1, wid), lambda b, j, i: (0, j))
    row = lambda t: t.reshape(1, d)
    return pl.pallas_call(
        functools.partial(_wkv_kernel, groups),
        grid=(bsz, d // wid, seq // ts),
        in_specs=[tile] * 6 + [par] * 5,
        out_specs=tile,
        out_shape=jax.ShapeDtypeStruct((bsz, seq, d), BF16),
        scratch_shapes=[pltpu.VMEM((groups, LANES, LANES), F32)],
        compiler_params=_cparams("parallel", "parallel", "arbitrary"),
        name="wkv7",
    )(r, k, v, ld, a, g, row(k_k), row(k_a), row(r_k), row(ln_w), row(ln_b))


FFN_HALO = 16
FFN_CHUNK = 256


def _chunk_gate_val(w, ff):
    lead = w.shape[:-1]
    return w.reshape(*lead, 2, ff // FFN_CHUNK, FFN_CHUNK).swapaxes(-3, -2).reshape(*lead, 2 * ff)


def _ffn_kernel(final, x_ref, xh_ref, y_ref, yh_ref, wo_ref, g1_ref, nw_ref, sh_ref, sc_ref, g2_ref, wup_ref, cw_ref,
                cb_ref, wd_ref, *rest):
    if final:
        fn_ref, o_ref, h_buf, u_buf, act_buf = rest
    else:
        o_ref, h_buf, u_buf, act_buf = rest
    i = pl.program_id(1)
    ts = x_ref.shape[1]
    fc = FFN_CHUNK
    nw, sh, sc, g1 = nw_ref[...], sh_ref[0], sc_ref[0], g1_ref[0]

    x1 = x_ref[0] + g1 * _dot(y_ref[0], wo_ref[...])
    x1h = xh_ref[0] + g1 * _dot(yh_ref[0], wo_ref[...])
    h_buf[0:FFN_HALO, :] = jnp.where(i > 0, _rms(x1h, nw) * (1.0 + sc) + sh, 0.0).astype(BF16)
    h_buf[FFN_HALO:, :] = (_rms(x1, nw) * (1.0 + sc) + sh).astype(BF16)

    for c in range(wd_ref.shape[0] // fc):
        cols = slice(2 * c * fc, 2 * (c + 1) * fc)
        slot = c % 2
        u_buf[slot] = _dot(h_buf[...], wup_ref[:, cols])
        u = (cb_ref[:, cols] + cw_ref[0:1, cols] * u_buf[slot, FFN_HALO - 2:FFN_HALO - 2 + ts, :]
             + cw_ref[1:2, cols] * u_buf[slot, FFN_HALO - 1:FFN_HALO - 1 + ts, :]
             + cw_ref[2:3, cols] * u_buf[slot, FFN_HALO:, :])
        gate, val = u[:, :fc], u[:, fc:]
        act_buf[:, c * fc:(c + 1) * fc] = (gate * _sigmoid(gate) * val).astype(BF16)

    out = x1 + g2_ref[0] * _dot(act_buf[...], wd_ref[...])
    if final:
        out = _rms(out, fn_ref[...])
    o_ref[0] = out


def _ffn(x, y, w_o, g1, nw, sh, sc, g2, w_up, conv_w, conv_b, w_down, final_nw, ts):
    bsz, seq, d = x.shape
    ff = w_down.shape[0]
    assert ff % FFN_CHUNK == 0
    final = final_nw is not None
    tile = lambda a: pl.BlockSpec((1, ts, a.shape[2]), lambda b, i: (b, i, 0))
    halo = lambda a: pl.BlockSpec((1, FFN_HALO, a.shape[2]),
                                  lambda b, i: (b, jnp.maximum(i * (ts // FFN_HALO) - 1, 0), 0))
    per_b = pl.BlockSpec((1, 1, d), lambda b, i: (b, 0, 0))
    once = lambda a: pl.BlockSpec(a.shape, lambda b, i: (0,) * a.ndim, pipeline_mode=pl.Buffered(1))
    nw = nw.reshape(1, d)
    w_up = _chunk_gate_val(w_up, ff)
    conv_w = _chunk_gate_val(conv_w, ff)
    conv_b = _chunk_gate_val(conv_b.reshape(1, -1), ff)
    args = [x, x, y, y, w_o, g1, nw, sh, sc, g2, w_up, conv_w, conv_b, w_down]
    specs = [tile(x), halo(x), tile(y), halo(y), once(w_o), per_b, once(nw), per_b, per_b, per_b, once(w_up),
             once(conv_w), once(conv_b), once(w_down)]
    if final:
        args.append(final_nw.reshape(1, d))
        specs.append(once(args[-1]))
    return pl.pallas_call(
        functools.partial(_ffn_kernel, final),
        grid=(bsz, seq // ts),
        in_specs=specs,
        out_specs=tile(x),
        out_shape=jax.ShapeDtypeStruct(x.shape, F32),
        scratch_shapes=[pltpu.VMEM((ts + FFN_HALO, d), BF16), pltpu.VMEM((2, ts + FFN_HALO, 2 * FFN_CHUNK), F32),
                        pltpu.VMEM((ts, ff), BF16)],
        compiler_params=_cparams("parallel", "parallel"),
        name="proj_conv_glu_ffn",
    )(*args)


def _rope_tables(seq):
    half = ROT_DIM // 2
    pos = jnp.arange(seq, dtype=F32)
    inv = ROPE_THETA ** (-jnp.arange(0, ROT_DIM, 2, dtype=F32) / ROT_DIM)
    ang = pos[:, None] * inv[None, :]
    cos, sin = jnp.cos(ang), jnp.sin(ang)
    ones = jnp.ones((seq, HEAD - ROT_DIM), F32)
    zeros = jnp.zeros((seq, HEAD - half), F32)
    cos_t = jnp.concatenate([cos, cos, ones], axis=1)
    sin_lo = jnp.concatenate([-sin, zeros], axis=1)
    sin_hi = jnp.concatenate([jnp.zeros((seq, half), F32), sin, zeros[:, half:]], axis=1)
    rep = lambda t: jnp.concatenate([t] * (LANES // HEAD), axis=1)
    return rep(cos_t), rep(sin_lo), rep(sin_hi)


def _rope(t, cos_t, sin_lo, sin_hi):
    half = ROT_DIM // 2
    outs = []
    for j in range(t.shape[1] // LANES):
        tj = t[:, j * LANES:(j + 1) * LANES]
        outs.append(tj * cos_t + pltpu.roll(tj, LANES - half, axis=1) * sin_lo + pltpu.roll(tj, half, axis=1) * sin_hi)
    return jnp.concatenate(outs, axis=1)


def _q_proj_kernel(x_ref, nw_ref, sh_ref, sc_ref, w_ref, cos_ref, sl_ref, sh2_ref, q_out):
    h = (_rms(x_ref[0], nw_ref[...]) * (1.0 + sc_ref[0]) + sh_ref[0]).astype(BF16)
    q = _rope(_dot(h, w_ref[...]), cos_ref[...], sl_ref[...], sh2_ref[...])
    q_out[0] = (q * (HEAD ** -0.5 * LOG2_E)).astype(q_out.dtype)


def _kv_proj_kernel(x_ref, nw_ref, w_ref, cos_ref, sl_ref, sh2_ref, k_out, v_out):
    h = _rms(x_ref[0], nw_ref[...]).astype(BF16)
    kv = _dot(h, w_ref[...])
    nk = k_out.shape[2]
    k_out[0] = _rope(kv[:, :nk], cos_ref[...], sl_ref[...], sh2_ref[...]).astype(k_out.dtype)
    v_out[0] = kv[:, nk:].astype(v_out.dtype)


def _q_proj(x, nw, sh, sc, w, tables, ts):
    bsz, seq, d = x.shape
    tile = pl.BlockSpec((1, ts, d), lambda b, i: (b, i, 0))
    per_b = pl.BlockSpec((1, 1, d), lambda b, i: (b, 0, 0))
    tab = pl.BlockSpec((ts, LANES), lambda b, i: (i, 0))
    return pl.pallas_call(
        _q_proj_kernel,
        grid=(bsz, seq // ts),
        in_specs=[tile, pl.BlockSpec((1, d), lambda b, i: (0, 0)), per_b, per_b,
                  pl.BlockSpec(w.shape, lambda b, i: (0, 0)), tab, tab, tab],
        out_specs=pl.BlockSpec((1, ts, w.shape[1]), lambda b, i: (b, i, 0)),
        out_shape=jax.ShapeDtypeStruct((bsz, seq, w.shape[1]), BF16),
        compiler_params=_cparams("parallel", "parallel"),
        name="q_proj",
    )(x, nw.reshape(1, d), sh, sc, w, *tables)


def _kv_proj(x, nw, w, qk_width, tables, ts):
    bsz, seq, d = x.shape
    v_width = w.shape[1] - qk_width
    tile = pl.BlockSpec((1, ts, d), lambda b, i: (b, i, 0))
    tab = pl.BlockSpec((ts, LANES), lambda b, i: (i, 0))
    return pl.pallas_call(
        _kv_proj_kernel,
        grid=(bsz, seq // ts),
        in_specs=[tile, pl.BlockSpec((1, d), lambda b, i: (0, 0)), pl.BlockSpec(w.shape, lambda b, i: (0, 0)),
                  tab, tab, tab],
        out_specs=[pl.BlockSpec((1, ts, qk_width), lambda b, i: (b, i, 0)),
                   pl.BlockSpec((1, ts, v_width), lambda b, i: (b, i, 0))],
        out_shape=[jax.ShapeDtypeStruct((bsz, seq, qk_width), BF16),
                   jax.ShapeDtypeStruct((bsz, seq, v_width), BF16)],
        compiler_params=_cparams("parallel", "parallel"),
        name="kv_proj",
    )(x, nw.reshape(1, d), w, *tables)


MASKED = -1e30


def _attn_kernel(lam_init, q_ref, k_ref, v_ref, lam_ref, sub_ref, o_ref):
    qi = pl.program_id(2)
    tq = q_ref.shape[1]
    q = q_ref[0]
    lane = lax.broadcasted_iota(jnp.int32, q.shape, 1)
    zero = jnp.zeros_like(q)
    qq = jnp.concatenate([jnp.where(lane < HEAD, q, zero), jnp.where(lane >= HEAD, q, zero)], axis=0)

    def step(j, carry, diagonal):
        m, l, acc = carry
        tok = pl.ds(pl.multiple_of(j * tq, tq), tq)
        s = _dot_nt(qq, k_ref[0, tok, :]).astype(BF16)
        if diagonal:
            qpos = lax.broadcasted_iota(jnp.int32, s.shape, 0) % tq
            kpos = lax.broadcasted_iota(jnp.int32, s.shape, 1)
            s = jnp.where(kpos <= qpos, s, MASKED)
        m_new = jnp.maximum(m, jnp.max(s, axis=1, keepdims=True))
        p = jnp.exp2(s - m_new)
        alpha = jnp.exp2((m - m_new).astype(F32))
        l = alpha * l + jnp.sum(p.astype(F32), axis=1, keepdims=True)
        acc = alpha * acc + _dot(p, v_ref[0, tok, :])
        return m_new, l, acc

    init = (jnp.full((2 * tq, 1), MASKED, BF16), jnp.zeros((2 * tq, 1), F32), jnp.zeros((2 * tq, 2 * HEAD), F32))
    carry = lax.fori_loop(0, qi, lambda j, c: step(j, c, False), init)
    _, l, acc = step(qi, carry, True)
    o = acc / l
    lv = lam_ref[...]
    lam = (jnp.exp(jnp.sum(lv[0:1] * lv[1:2], axis=1, keepdims=True))
           - jnp.exp(jnp.sum(lv[2:3] * lv[3:4], axis=1, keepdims=True)) + lam_init)
    dlt = o[:tq] - lam * o[tq:]
    o_ref[0] = (_rms(dlt, sub_ref[...]) * (1.0 - lam_init)).astype(o_ref.dtype)


def _diff_attn(q, k, v, lam_vecs, subln, lam_init, tq):
    bsz, seq, d = q.shape
    wid = 2 * HEAD
    return pl.pallas_call(
        functools.partial(_attn_kernel, lam_init),
        grid=(bsz, d // wid, seq // tq),
        in_specs=[pl.BlockSpec((1, tq, wid), lambda b, h, i: (b, i, h)),
                  pl.BlockSpec((1, seq, wid), lambda b, h, i: (b, 0, h)),
                  pl.BlockSpec((1, seq, wid), lambda b, h, i: (b, 0, h)),
                  pl.BlockSpec(lam_vecs.shape, lambda b, h, i: (0, 0)),
                  pl.BlockSpec((1, wid), lambda b, h, i: (0, 0))],
        out_specs=pl.BlockSpec((1, tq, wid), lambda b, h, i: (b, i, h)),
        out_shape=jax.ShapeDtypeStruct((bsz, seq, d), BF16),
        compiler_params=_cparams("parallel", "parallel", "arbitrary"),
        name="diff_attn",
    )(q, k, v, lam_vecs, subln.reshape(1, wid))


def kernel(x, c, ada_w, ada_b, norm1, norm2, final_norm, a_mu, a_w_rkv, a_w0, a_w1, a_w2, a_a0, a_a1, a_a2, a_v0, a_v1, a_v2, a_g1, a_g2, a_k_k, a_k_a, a_r_k, a_ln_w, a_ln_b, a_w_o, kv_norm, w_kv, b_w_q, b_lam, b_subln, b_w_o, ffn_w_up, ffn_conv_w, ffn_conv_b, ffn_w_down):
    bsz, seq, d = x.shape
    depth = ada_w.shape[0]
    n_a = a_mu.shape[0]
    qk_width = b_w_q.shape[2]
    t = _tiles(seq)
    bf = lambda w: w.astype(BF16)
    tables = _rope_tables(seq)

    mod = _ada_mod(c, ada_w, ada_b)
    v_first = None
    k_sh = v_sh = None
    for l in range(depth):
        sh1, sc1, g1, sh2, sc2, g2 = (mod[l, :, j * d:(j + 1) * d].reshape(bsz, 1, d) for j in range(6))
        if l < n_a:
            vres = None if l == 0 else (a_v0[l - 1], bf(a_v1[l - 1]), bf(a_v2[l - 1]), v_first)
            r, k, v, ld, a, g = _rwkv_proj(
                x, norm1[l], sh1, sc1, a_mu[l], bf(a_w_rkv[l]), a_w0[l], bf(a_w1[l]), bf(a_w2[l]), a_a0[l],
                bf(a_a1[l]), bf(a_a2[l]), bf(a_g1[l]), bf(a_g2[l]), vres, t["proj"])
            if l == 0:
                v_first = v
            y = _wkv(r, k, v, ld, a, g, a_k_k[l], a_k_a[l], a_r_k[l], a_ln_w[l], a_ln_b[l], t["wkv"])
            w_o = a_w_o[l]
        else:
            j = l - n_a
            if j == 0:
                k_sh, v_sh = _kv_proj(x, kv_norm, bf(w_kv), qk_width, tables, t["qkv"])
            lam_init = 0.8 - 0.6 * math.exp(-0.3 * l)
            q = _q_proj(x, norm1[l], sh1, sc1, bf(b_w_q[j]), tables, t["qkv"])
            y = _diff_attn(q, k_sh, v_sh, b_lam[j], b_subln[j], lam_init, t["attn"])
            w_o = b_w_o[j]
        x = _ffn(x, y, bf(w_o), g1, norm2[l], sh2, sc2, g2, bf(ffn_w_up[l]), ffn_conv_w[l], ffn_conv_b[l],
                 bf(ffn_w_down[l]), final_norm if l == depth - 1 else None, t["ffn"])
    return x
```

```python
import functools
import math

import jax
import jax.numpy as jnp
from jax import lax
from jax.experimental import pallas as pl
from jax.experimental.pallas import tpu as pltpu

F32 = jnp.float32
BF16 = jnp.bfloat16
HIGHEST = lax.Precision.HIGHEST

NORM_EPS = 1e-6
GN_EPS = 64e-5
ROPE_THETA = 500000.0
LOG2_E = math.log2(math.e)
HEAD = 64
ROT_DIM = HEAD // 4
LANES = 128
WKV_CHUNK = 64
WKV_HEADS_PER_GROUP = LANES // HEAD
VMEM_LIMIT = 56 * 1024 * 1024


def _tiles(seq):
    def fit(t):
        t = min(t, seq)
        assert seq % t == 0
        return t
    return dict(proj=fit(512), wkv=fit(512), ffn=fit(512), qkv=fit(512), attn=fit(512))


def _cparams(*sem):
    return pltpu.CompilerParams(dimension_semantics=sem, vmem_limit_bytes=VMEM_LIMIT)


def _dot(a, b, **kw):
    return jnp.dot(a, b, preferred_element_type=F32, **kw)


def _dot_nt(a, b):
    return lax.dot_general(a, b, (((1,), (1,)), ((), ())), preferred_element_type=F32)


def _dot_tn(a, b):
    return lax.dot_general(a, b, (((0,), (0,)), ((), ())), preferred_element_type=F32)


def _rms(x, g):
    return x * lax.rsqrt(jnp.mean(x * x, axis=-1, keepdims=True) + NORM_EPS) * g


def _sigmoid(x):
    return 1.0 / (1.0 + jnp.exp(-x))


def _softplus(x):
    return jnp.maximum(x, 0.0) + jnp.log1p(jnp.exp(-jnp.abs(x)))


def _mod_kernel(c_ref, w_ref, b_ref, o_ref):
    c = c_ref[...]
    o_ref[0] = _dot(c * _sigmoid(c), w_ref[0], precision=HIGHEST) + b_ref[0]


def _ada_mod(c, ada_w, ada_b):
    depth, d, n = ada_w.shape
    bsz = c.shape[0]
    tn = d
    return pl.pallas_call(
        _mod_kernel,
        grid=(depth, n // tn),
        in_specs=[pl.BlockSpec((bsz, d), lambda l, j: (0, 0)),
                  pl.BlockSpec((1, d, tn), lambda l, j: (l, 0, j)),
                  pl.BlockSpec((1, 1, tn), lambda l, j: (l, 0, j))],
        out_specs=pl.BlockSpec((1, bsz, tn), lambda l, j: (l, 0, j)),
        out_shape=jax.ShapeDtypeStruct((depth, bsz, n), F32),
        compiler_params=_cparams("parallel", "parallel"),
        name="ada_mod",
    )(c, ada_w, ada_b.reshape(depth, 1, n))


def _rwkv_proj_kernel(has_vres, x_ref, xh_ref, nw_ref, sh_ref, sc_ref, mu_ref, wrkv_ref,
                      w0_ref, w1_ref, w2_ref, a0_ref, a1_ref, a2_ref, g1_ref, g2_ref, *rest):
    if has_vres:
        v0_ref, v1_ref, v2_ref, vf_ref = rest[:4]
        rest = rest[4:]
    r_out, k_out, v_out, ld_out, a_out, g_out = rest
    i = pl.program_id(1)
    nw, sh, sc = nw_ref[...], sh_ref[0], sc_ref[0]

    def prep(xv):
        return _rms(xv, nw) * (1.0 + sc) + sh

    h = prep(x_ref[0])
    halo = prep(xh_ref[0])
    prev_last = jnp.where(i > 0, halo[7:8, :], 0.0)
    row = lax.broadcasted_iota(jnp.int32, h.shape, 0)
    hprev = jnp.where(row == 0, prev_last, pltpu.roll(h, 1, axis=0))
    xx = hprev - h

    def mix(j):
        return (h + xx * mu_ref[j:j + 1, :]).astype(BF16)

    xr, xw, xk, xv, xa, xg = (mix(j) for j in range(6))
    r_out[0] = _dot(xr, wrkv_ref[0])
    k_out[0] = _dot(xk, wrkv_ref[1])
    v = _dot(xv, wrkv_ref[2])
    if has_vres:
        gate = _sigmoid(v0_ref[...] + _dot(_dot(xv, v1_ref[...]).astype(BF16), v2_ref[...]))
        v = v + (vf_ref[0] - v) * gate
    v_out[0] = v
    wl = w0_ref[...] + _dot(jnp.tanh(_dot(xw, w1_ref[...])).astype(BF16), w2_ref[...])
    ld_out[0] = -jnp.exp(-_softplus(-wl) - 0.5)
    a_out[0] = _sigmoid(a0_ref[...] + _dot(_dot(xa, a1_ref[...]).astype(BF16), a2_ref[...]))
    g_out[0] = _dot(_sigmoid(_dot(xg, g1_ref[...])).astype(BF16), g2_ref[...]).astype(g_out.dtype)


def _rwkv_proj(x, nw, sh, sc, mu, wrkv, w0, w1, w2, a0, a1, a2, g1, g2, vres, ts):
    bsz, seq, d = x.shape
    has_vres = vres is not None
    row = lambda a: a.reshape(1, -1)
    full = lambda a: pl.BlockSpec(a.shape, lambda b, i: (0,) * a.ndim, pipeline_mode=pl.Buffered(1))
    tile = pl.BlockSpec((1, ts, d), lambda b, i: (b, i, 0))
    halo = pl.BlockSpec((1, 8, d), lambda b, i: (b, jnp.maximum(i * (ts // 8) - 1, 0), 0))
    per_b = pl.BlockSpec((1, 1, d), lambda b, i: (b, 0, 0))
    args = [x, x, row(nw), sh, sc, mu, wrkv, row(w0), w1, w2, row(a0), a1, a2, g1, g2]
    specs = [tile, halo, full(row(nw)), per_b, per_b, full(mu), full(wrkv), full(row(w0)), full(w1),
             full(w2), full(row(a0)), full(a1), full(a2), full(g1), full(g2)]
    if has_vres:
        v0, v1, v2, vf = vres
        args += [row(v0), v1, v2, vf]
        specs += [full(row(v0)), full(v1), full(v2), tile]
    out = jax.ShapeDtypeStruct((bsz, seq, d), F32)
    return pl.pallas_call(
        functools.partial(_rwkv_proj_kernel, has_vres),
        grid=(bsz, seq // ts),
        in_specs=specs,
        out_specs=[tile] * 6,
        out_shape=[out] * 5 + [jax.ShapeDtypeStruct((bsz, seq, d), BF16)],
        compiler_params=_cparams("parallel", "arbitrary"),
        name="rwkv_proj",
    )(*args)


def _wkv_consts(c, w):
    heads = w // HEAD
    ti = lax.broadcasted_iota(jnp.int32, (c, 3 * c), 0)
    tj = lax.broadcasted_iota(jnp.int32, (c, 3 * c), 1) % c
    t = lax.broadcasted_iota(jnp.int32, (c, 2 * heads * c), 0)
    s = lax.broadcasted_iota(jnp.int32, (c, 2 * heads * c), 1) % c
    vi = lax.broadcasted_iota(jnp.int32, (w, w), 0) // HEAD
    ki = lax.broadcasted_iota(jnp.int32, (w, w), 1) // HEAD
    return dict(
        tril3=(ti >= tj).astype(BF16),
        lane_head=lax.broadcasted_iota(jnp.int32, (c, w), 1) // HEAD,
        col_head=lax.broadcasted_iota(jnp.int32, (c, heads * c), 1) // c,
        strict=s < t, incl=s <= t, same_head=vi == ki)


def _wkv_chunk(inputs, params, states, consts):
    c, w = inputs[0][0].shape
    heads = w // HEAD
    rows = heads * c
    lane_head, col_head = consts["lane_head"], consts["col_head"]
    strict, incl, same_head = consts["strict"], consts["incl"], consts["same_head"]

    def stack(x):
        return jnp.concatenate([jnp.where(lane_head == hd, x, 0.0) for hd in range(heads)], axis=0)

    def segsum(x):
        out = jnp.sum(jnp.where(lane_head == 0, x, 0.0), axis=1, keepdims=True)
        for hd in range(1, heads):
            out = jnp.where(lane_head == hd, jnp.sum(jnp.where(lane_head == hd, x, 0.0), axis=1, keepdims=True), out)
        return out

    def cumulative(ld):
        ld_hi = ld.astype(BF16)
        rem = ld - ld_hi.astype(F32)
        ld_mid = rem.astype(BF16)
        ld_lo = (rem - ld_mid.astype(F32)).astype(BF16)
        return _dot(consts["tril3"], jnp.concatenate([ld_hi, ld_mid, ld_lo], axis=0))

    def prepare(inp, par, cum):
        r, k, v, ld, a, _ = inp
        k_k, k_a = par[0], par[1]
        last = cum[c - 1:c, :]
        g_in, g_ex, g_inv, g_end = jnp.exp(cum), jnp.exp(cum - ld), jnp.exp(-cum), jnp.exp(last - cum)
        kk = k * k_k
        kk = kk / jnp.maximum(jnp.sqrt(segsum(kk * kk)), 1e-12)
        k2 = k * (1.0 + (a - 1.0) * k_a)
        b_in = kk * a
        return dict(
            k2=k2, decay=jnp.exp(last), vs=stack(v).astype(BF16),
            lhs=jnp.concatenate([-kk * g_ex, r * g_in], axis=0).astype(BF16),
            rhs=jnp.concatenate([stack(b_in * g_inv), stack(k2 * g_inv)], axis=0).astype(BF16),
            bk_end=jnp.concatenate([b_in * g_end, k2 * g_end], axis=0).astype(BF16))

    cums = [cumulative(inp[3]) for inp in inputs]
    pre = [prepare(inp, par, cum) for inp, par, cum in zip(inputs, params, cums)]
    aas = [_dot_nt(p["lhs"], p["rhs"]) for p in pre]
    pss = [_dot_nt(p["lhs"], st.astype(BF16)) for p, st in zip(pre, states)]
    a_aks = [jnp.where(strict[:, :rows], aa[:c, rows:], 0.0).astype(BF16) for aa in aas]
    a_rs = [jnp.where(incl, aa[c:, :], 0.0).astype(BF16) for aa in aas]

    def block_diag(aa):
        n = jnp.where(strict[:, :rows], aa[:c, :rows], 0.0)
        return jnp.concatenate([jnp.where(col_head == hd, n, 0.0) for hd in range(heads)], axis=0).astype(BF16)

    ns = [block_diag(aa) for aa in aas]
    us = [stack(ps[:c] + _dot(a_ak, p["vs"])) for ps, a_ak, p in zip(pss, a_aks, pre)]
    steps = int(math.log2(c))
    for i in range(steps):
        if i < steps - 1:
            res = [_dot(n, jnp.concatenate([u.astype(BF16), n], axis=1)) for n, u in zip(ns, us)]
            us = [u + rs[:, :w] for u, rs in zip(us, res)]
            ns = [rs[:, w:].astype(BF16) for rs in res]
        else:
            us = [u + _dot(n, u.astype(BF16)) for n, u in zip(ns, us)]

    uvs = [jnp.concatenate([u.astype(BF16), p["vs"]], axis=0) for u, p in zip(us, pre)]
    ys = [ps[c:] + _dot(a_r, uv) for ps, a_r, uv in zip(pss, a_rs, uvs)]

    def next_state(inp, st, p, u):
        u_tok = u[:c]
        for hd in range(1, heads):
            u_tok = u_tok + u[hd * c:(hd + 1) * c]
        uv_tok = jnp.concatenate([u_tok, inp[2]], axis=0).astype(BF16)
        return st * p["decay"] + jnp.where(same_head, _dot_tn(uv_tok, p["bk_end"]), 0.0)

    new_states = [next_state(inp, st, p, u) for inp, st, p, u in zip(inputs, states, pre, us)]

    def finish(inp, par, p, y):
        r, _, v, _, _, g = inp
        r_k, ln_w, ln_b = par[2], par[3], par[4]
        mean = segsum(y) * (1.0 / HEAD)
        dlt = y - mean
        var = segsum(dlt * dlt) * (1.0 / HEAD)
        out = dlt * lax.rsqrt(var + GN_EPS) * ln_w + ln_b + segsum(r * p["k2"] * r_k) * v
        return out * g

    outs = [finish(inp, par, p, y) for inp, par, p, y in zip(inputs, params, pre, ys)]
    return outs, new_states


def _wkv_kernel(groups, r_ref, k_ref, v_ref, ld_ref, a_ref, g_ref, kk_ref, ka_ref, rk_ref, lnw_ref, lnb_ref,
                o_ref, s_ref):
    @pl.when(pl.program_id(2) == 0)
    def _():
        s_ref[...] = jnp.zeros_like(s_ref)

    n_chunks = r_ref.shape[1] // WKV_CHUNK
    consts = _wkv_consts(WKV_CHUNK, LANES)
    lanes = [pl.ds(gi * LANES, LANES) for gi in range(groups)]

    def body(ci, carry):
        tok = pl.ds(pl.multiple_of(ci * WKV_CHUNK, WKV_CHUNK), WKV_CHUNK)
        inputs = [(r_ref[0, tok, ln], k_ref[0, tok, ln], v_ref[0, tok, ln], ld_ref[0, tok, ln], a_ref[0, tok, ln],
                   g_ref[0, tok, ln].astype(F32)) for ln in lanes]
        params = [(kk_ref[:, ln], ka_ref[:, ln], rk_ref[:, ln], lnw_ref[:, ln], lnb_ref[:, ln]) for ln in lanes]
        outs, new_states = _wkv_chunk(inputs, params, [s_ref[gi] for gi in range(groups)], consts)
        for gi, ln in enumerate(lanes):
            s_ref[gi] = new_states[gi]
            o_ref[0, tok, ln] = outs[gi].astype(o_ref.dtype)
        return carry

    lax.fori_loop(0, n_chunks, body, 0)


def _wkv(r, k, v, ld, a, g, k_k, k_a, r_k, ln_w, ln_b, ts, groups=8):
    bsz, seq, d = r.shape
    wid = groups * LANES
    tile = pl.BlockSpec((1, ts, wid), lambda b, j, i: (b, i, j))
    par = pl.BlockSpec((1, wid), lambda b, j, i: (0, j))
    row = lambda t: t.reshape(1, d)
    return pl.pallas_call(
        functools.partial(_wkv_kernel, groups),
        grid=(bsz, d // wid, seq // ts),
        in_specs=[tile] * 6 + [par] * 5,
        out_specs=tile,
        out_shape=jax.ShapeDtypeStruct((bsz, seq, d), BF16),
        scratch_shapes=[pltpu.VMEM((groups, LANES, LANES), F32)],
        compiler_params=_cparams("parallel", "parallel", "arbitrary"),
        name="wkv7",
    )(r, k, v, ld, a, g, row(k_k), row(k_a), row(r_k), row(ln_w), row(ln_b))


FFN_HALO = 16
FFN_CHUNK = 256


def _ffn_kernel(final, x_ref, xh_ref, y_ref, yh_ref, wo_ref, g1_ref, nw_ref, sh_ref, sc_ref, g2_ref, wup_ref, cw_ref,
                cb_ref, wd_ref, *rest):
    if final:
        fn_ref, o_ref, h_buf, u_buf, act_buf = rest
    else:
        o_ref, h_buf, u_buf, act_buf = rest
    i = pl.program_id(1)
    ts = x_ref.shape[1]
    fc = FFN_CHUNK
    nw, sh, sc, g1 = nw_ref[...], sh_ref[0], sc_ref[0], g1_ref[0]

    x1 = x_ref[0] + g1 * _dot(y_ref[0], wo_ref[...])
    x1h = xh_ref[0] + g1 * _dot(yh_ref[0], wo_ref[...])
    h_buf[0:FFN_HALO, :] = jnp.where(i > 0, _rms(x1h, nw) * (1.0 + sc) + sh, 0.0).astype(BF16)
    h_buf[FFN_HALO:, :] = (_rms(x1, nw) * (1.0 + sc) + sh).astype(BF16)

    ff = wd_ref.shape[0]
    for c in range(ff // fc):
        slot = c % 2
        halves = []
        for cols, lanes in ((slice(c * fc, (c + 1) * fc), slice(0, fc)),
                            (slice(ff + c * fc, ff + (c + 1) * fc), slice(fc, 2 * fc))):
            u_buf[slot, :, lanes] = _dot(h_buf[...], wup_ref[:, cols])
            halves.append(cb_ref[:, cols] + cw_ref[0:1, cols] * u_buf[slot, FFN_HALO - 2:FFN_HALO - 2 + ts, lanes]
                          + cw_ref[1:2, cols] * u_buf[slot, FFN_HALO - 1:FFN_HALO - 1 + ts, lanes]
                          + cw_ref[2:3, cols] * u_buf[slot, FFN_HALO:, lanes])
        gate, val = halves
        act_buf[:, c * fc:(c + 1) * fc] = (gate * _sigmoid(gate) * val).astype(BF16)

    out = x1 + g2_ref[0] * _dot(act_buf[...], wd_ref[...])
    if final:
        out = _rms(out, fn_ref[...])
    o_ref[0] = out


def _ffn(x, y, w_o, g1, nw, sh, sc, g2, w_up, conv_w, conv_b, w_down, final_nw, ts):
    bsz, seq, d = x.shape
    ff = w_down.shape[0]
    assert ff % FFN_CHUNK == 0 and ff % LANES == 0
    final = final_nw is not None
    tile = lambda a: pl.BlockSpec((1, ts, a.shape[2]), lambda b, i: (b, i, 0))
    halo = lambda a: pl.BlockSpec((1, FFN_HALO, a.shape[2]),
                                  lambda b, i: (b, jnp.maximum(i * (ts // FFN_HALO) - 1, 0), 0))
    per_b = pl.BlockSpec((1, 1, d), lambda b, i: (b, 0, 0))
    once = lambda a: pl.BlockSpec(a.shape, lambda b, i: (0,) * a.ndim, pipeline_mode=pl.Buffered(1))
    nw = nw.reshape(1, d)
    conv_b = conv_b.reshape(1, -1)
    args = [x, x, y, y, w_o, g1, nw, sh, sc, g2, w_up, conv_w, conv_b, w_down]
    specs = [tile(x), halo(x), tile(y), halo(y), once(w_o), per_b, once(nw), per_b, per_b, per_b, once(w_up),
             once(conv_w), once(conv_b), once(w_down)]
    if final:
        args.append(final_nw.reshape(1, d))
        specs.append(once(args[-1]))
    return pl.pallas_call(
        functools.partial(_ffn_kernel, final),
        grid=(bsz, seq // ts),
        in_specs=specs,
        out_specs=tile(x),
        out_shape=jax.ShapeDtypeStruct(x.shape, F32),
        scratch_shapes=[pltpu.VMEM((ts + FFN_HALO, d), BF16), pltpu.VMEM((2, ts + FFN_HALO, 2 * FFN_CHUNK), F32),
                        pltpu.VMEM((ts, ff), BF16)],
        compiler_params=_cparams("parallel", "parallel"),
        name="proj_conv_glu_ffn",
    )(*args)


def _rope_tables(seq):
    half = ROT_DIM // 2
    pos = jnp.arange(seq, dtype=F32)
    inv = ROPE_THETA ** (-jnp.arange(0, ROT_DIM, 2, dtype=F32) / ROT_DIM)
    ang = pos[:, None] * inv[None, :]
    cos, sin = jnp.cos(ang), jnp.sin(ang)
    ones = jnp.ones((seq, HEAD - ROT_DIM), F32)
    zeros = jnp.zeros((seq, HEAD - half), F32)
    cos_t = jnp.concatenate([cos, cos, ones], axis=1)
    sin_lo = jnp.concatenate([-sin, zeros], axis=1)
    sin_hi = jnp.concatenate([jnp.zeros((seq, half), F32), sin, zeros[:, half:]], axis=1)
    rep = lambda t: jnp.concatenate([t] * (LANES // HEAD), axis=1)
    return rep(cos_t), rep(sin_lo), rep(sin_hi)


def _rope(t, cos_t, sin_lo, sin_hi):
    half = ROT_DIM // 2
    outs = []
    for j in range(t.shape[1] // LANES):
        tj = t[:, j * LANES:(j + 1) * LANES]
        outs.append(tj * cos_t + pltpu.roll(tj, LANES - half, axis=1) * sin_lo + pltpu.roll(tj, half, axis=1) * sin_hi)
    return jnp.concatenate(outs, axis=1)


def _q_proj_kernel(x_ref, nw_ref, sh_ref, sc_ref, w_ref, cos_ref, sl_ref, sh2_ref, q_out):
    h = (_rms(x_ref[0], nw_ref[...]) * (1.0 + sc_ref[0]) + sh_ref[0]).astype(BF16)
    q = _rope(_dot(h, w_ref[...]), cos_ref[...], sl_ref[...], sh2_ref[...])
    q_out[0] = (q * (HEAD ** -0.5 * LOG2_E)).astype(q_out.dtype)


def _kv_proj_kernel(x_ref, nw_ref, w_ref, cos_ref, sl_ref, sh2_ref, k_out, v_out):
    h = _rms(x_ref[0], nw_ref[...]).astype(BF16)
    kv = _dot(h, w_ref[...])
    nk = k_out.shape[2]
    k_out[0] = _rope(kv[:, :nk], cos_ref[...], sl_ref[...], sh2_ref[...]).astype(k_out.dtype)
    v_out[0] = kv[:, nk:].astype(v_out.dtype)


def _q_proj(x, nw, sh, sc, w, tables, ts):
    bsz, seq, d = x.shape
    tile = pl.BlockSpec((1, ts, d), lambda b, i: (b, i, 0))
    per_b = pl.BlockSpec((1, 1, d), lambda b, i: (b, 0, 0))
    tab = pl.BlockSpec((ts, LANES), lambda b, i: (i, 0))
    return pl.pallas_call(
        _q_proj_kernel,
        grid=(bsz, seq // ts),
        in_specs=[tile, pl.BlockSpec((1, d), lambda b, i: (0, 0)), per_b, per_b,
                  pl.BlockSpec(w.shape, lambda b, i: (0, 0)), tab, tab, tab],
        out_specs=pl.BlockSpec((1, ts, w.shape[1]), lambda b, i: (b, i, 0)),
        out_shape=jax.ShapeDtypeStruct((bsz, seq, w.shape[1]), BF16),
        compiler_params=_cparams("parallel", "parallel"),
        name="q_proj",
    )(x, nw.reshape(1, d), sh, sc, w, *tables)


def _kv_proj(x, nw, w, qk_width, tables, ts):
    bsz, seq, d = x.shape
    v_width = w.shape[1] - qk_width
    tile = pl.BlockSpec((1, ts, d), lambda b, i: (b, i, 0))
    tab = pl.BlockSpec((ts, LANES), lambda b, i: (i, 0))
    return pl.pallas_call(
        _kv_proj_kernel,
        grid=(bsz, seq // ts),
        in_specs=[tile, pl.BlockSpec((1, d), lambda b, i: (0, 0)), pl.BlockSpec(w.shape, lambda b, i: (0, 0)),
                  tab, tab, tab],
        out_specs=[pl.BlockSpec((1, ts, qk_width), lambda b, i: (b, i, 0)),
                   pl.BlockSpec((1, ts, v_width), lambda b, i: (b, i, 0))],
        out_shape=[jax.ShapeDtypeStruct((bsz, seq, qk_width), BF16),
                   jax.ShapeDtypeStruct((bsz, seq, v_width), BF16)],
        compiler_params=_cparams("parallel", "parallel"),
        name="kv_proj",
    )(x, nw.reshape(1, d), w, *tables)


MASKED = -1e30


def _attn_kernel(lam_init, q_ref, k_ref, v_ref, lam_ref, sub_ref, o_ref):
    qi = pl.program_id(2)
    tq = q_ref.shape[1]
    q = q_ref[0]
    lane = lax.broadcasted_iota(jnp.int32, q.shape, 1)
    zero = jnp.zeros_like(q)
    qq = jnp.concatenate([jnp.where(lane < HEAD, q, zero), jnp.where(lane >= HEAD, q, zero)], axis=0)

    def step(j, carry, diagonal):
        m, l, acc = carry
        tok = pl.ds(pl.multiple_of(j * tq, tq), tq)
        s = _dot_nt(qq, k_ref[0, tok, :]).astype(BF16)
        if diagonal:
            qpos = lax.broadcasted_iota(jnp.int32, s.shape, 0) % tq
            kpos = lax.broadcasted_iota(jnp.int32, s.shape, 1)
            s = jnp.where(kpos <= qpos, s, MASKED)
        m_new = jnp.maximum(m, jnp.max(s, axis=1, keepdims=True))
        p = jnp.exp2(s - m_new)
        alpha = jnp.exp2((m - m_new).astype(F32))
        l = alpha * l + jnp.sum(p.astype(F32), axis=1, keepdims=True)
        acc = alpha * acc + _dot(p, v_ref[0, tok, :])
        return m_new, l, acc

    init = (jnp.full((2 * tq, 1), MASKED, BF16), jnp.zeros((2 * tq, 1), F32), jnp.zeros((2 * tq, 2 * HEAD), F32))
    carry = lax.fori_loop(0, qi, lambda j, c: step(j, c, False), init)
    _, l, acc = step(qi, carry, True)
    o = acc / l
    lv = lam_ref[...]
    lam = (jnp.exp(jnp.sum(lv[0:1] * lv[1:2], axis=1, keepdims=True))
           - jnp.exp(jnp.sum(lv[2:3] * lv[3:4], axis=1, keepdims=True)) + lam_init)
    dlt = o[:tq] - lam * o[tq:]
    o_ref[0] = (_rms(dlt, sub_ref[...]) * (1.0 - lam_init)).astype(o_ref.dtype)


def _diff_attn(q, k, v, lam_vecs, subln, lam_init, tq):
    bsz, seq, d = q.shape
    wid = 2 * HEAD
    return pl.pallas_call(
        functools.partial(_attn_kernel, lam_init),
        grid=(bsz, d // wid, seq // tq),
        in_specs=[pl.BlockSpec((1, tq, wid), lambda b, h, i: (b, i, h)),
                  pl.BlockSpec((1, seq, wid), lambda b, h, i: (b, 0, h)),
                  pl.BlockSpec((1, seq, wid), lambda b, h, i: (b, 0, h)),
                  pl.BlockSpec(lam_vecs.shape, lambda b, h, i: (0, 0)),
                  pl.BlockSpec((1, wid), lambda b, h, i: (0, 0))],
        out_specs=pl.BlockSpec((1, tq, wid), lambda b, h, i: (b, i, h)),
        out_shape=jax.ShapeDtypeStruct((bsz, seq, d), BF16),
        compiler_params=_cparams("parallel", "parallel", "arbitrary"),
        name="diff_attn",
    )(q, k, v, lam_vecs, subln.reshape(1, wid))


def kernel(x, c, ada_w, ada_b, norm1, norm2, final_norm, a_mu, a_w_rkv, a_w0, a_w1, a_w2, a_a0, a_a1, a_a2, a_v0, a_v1, a_v2, a_g1, a_g2, a_k_k, a_k_a, a_r_k, a_ln_w, a_ln_b, a_w_o, kv_norm, w_kv, b_w_q, b_lam, b_subln, b_w_o, ffn_w_up, ffn_conv_w, ffn_conv_b, ffn_w_down):
    bsz, seq, d = x.shape
    depth = ada_w.shape[0]
    n_a = a_mu.shape[0]
    qk_width = b_w_q.shape[2]
    t = _tiles(seq)
    bf = lambda w: w.astype(BF16)
    tables = _rope_tables(seq)

    mod = _ada_mod(c, ada_w, ada_b)
    v_first = None
    k_sh = v_sh = None
    for l in range(depth):
        sh1, sc1, g1, sh2, sc2, g2 = (mod[l, :, j * d:(j + 1) * d].reshape(bsz, 1, d) for j in range(6))
        if l < n_a:
            vres = None if l == 0 else (a_v0[l - 1], bf(a_v1[l - 1]), bf(a_v2[l - 1]), v_first)
            r, k, v, ld, a, g = _rwkv_proj(
                x, norm1[l], sh1, sc1, a_mu[l], bf(a_w_rkv[l]), a_w0[l], bf(a_w1[l]), bf(a_w2[l]), a_a0[l],
                bf(a_a1[l]), bf(a_a2[l]), bf(a_g1[l]), bf(a_g2[l]), vres, t["proj"])
            if l == 0:
                v_first = v
            y = _wkv(r, k, v, ld, a, g, a_k_k[l], a_k_a[l], a_r_k[l], a_ln_w[l], a_ln_b[l], t["wkv"])
            w_o = a_w_o[l]
        else:
            j = l - n_a
            if j == 0:
                k_sh, v_sh = _kv_proj(x, kv_norm, bf(w_kv), qk_width, tables, t["qkv"])
            lam_init = 0.8 - 0.6 * math.exp(-0.3 * l)
            q = _q_proj(x, norm1[l], sh1, sc1, bf(b_w_q[j]), tables, t["qkv"])
            y = _diff_attn(q, k_sh, v_sh, b_lam[j], b_subln[j], lam_init, t["attn"])
            w_o = b_w_o[j]
        x = _ffn(x, y, bf(w_o), g1, norm2[l], sh2, sc2, g2, bf(ffn_w_up[l]), ffn_conv_w[l], ffn_conv_b[l],
                 bf(ffn_w_down[l]), final_norm if l == depth - 1 else None, t["ffn"])
    return x
```

```python
import functools
import math

import jax
import jax.numpy as jnp
from jax import lax
from jax.experimental import pallas as pl
from jax.experimental.pallas import tpu as pltpu

F32 = jnp.float32
BF16 = jnp.bfloat16
HIGHEST = lax.Precision.HIGHEST

NORM_EPS = 1e-6
GN_EPS = 64e-5
ROPE_THETA = 500000.0
LOG2_E = math.log2(math.e)
HEAD = 64
ROT_DIM = HEAD // 4
LANES = 128
WKV_CHUNK = 64
WKV_HEADS_PER_GROUP = LANES // HEAD
VMEM_LIMIT = 56 * 1024 * 1024


def _tiles(seq):
    def fit(t):
        t = min(t, seq)
        assert seq % t == 0
        return t
    return dict(proj=fit(512), wkv=fit(256), ffn=fit(512), qkv=fit(512), attn=fit(512))


def _cparams(*sem):
    return pltpu.CompilerParams(dimension_semantics=sem, vmem_limit_bytes=VMEM_LIMIT)


def _dot(a, b, **kw):
    return jnp.dot(a, b, preferred_element_type=F32, **kw)


def _dot_nt(a, b):
    return lax.dot_general(a, b, (((1,), (1,)), ((), ())), preferred_element_type=F32)


def _dot_tn(a, b):
    return lax.dot_general(a, b, (((0,), (0,)), ((), ())), preferred_element_type=F32)


def _rms(x, g):
    return x * lax.rsqrt(jnp.mean(x * x, axis=-1, keepdims=True) + NORM_EPS) * g


def _sigmoid(x):
    return 1.0 / (1.0 + jnp.exp(-x))


def _softplus(x):
    return jnp.maximum(x, 0.0) + jnp.log1p(jnp.exp(-jnp.abs(x)))


def _mod_kernel(c_ref, w_ref, b_ref, o_ref):
    c = c_ref[...]
    o_ref[0] = _dot(c * _sigmoid(c), w_ref[0], precision=HIGHEST) + b_ref[0]


def _ada_mod(c, ada_w, ada_b):
    depth, d, n = ada_w.shape
    bsz = c.shape[0]
    tn = d
    return pl.pallas_call(
        _mod_kernel,
        grid=(depth, n // tn),
        in_specs=[pl.BlockSpec((bsz, d), lambda l, j: (0, 0)),
                  pl.BlockSpec((1, d, tn), lambda l, j: (l, 0, j)),
                  pl.BlockSpec((1, 1, tn), lambda l, j: (l, 0, j))],
        out_specs=pl.BlockSpec((1, bsz, tn), lambda l, j: (l, 0, j)),
        out_shape=jax.ShapeDtypeStruct((depth, bsz, n), F32),
        compiler_params=_cparams("parallel", "parallel"),
        name="ada_mod",
    )(c, ada_w, ada_b.reshape(depth, 1, n))


def _rwkv_proj_kernel(has_vres, x_ref, xh_ref, nw_ref, sh_ref, sc_ref, mu_ref, wrkv_ref,
                      w0_ref, w1_ref, w2_ref, a0_ref, a1_ref, a2_ref, g1_ref, g2_ref, *rest):
    if has_vres:
        v0_ref, v1_ref, v2_ref, vf_ref = rest[:4]
        rest = rest[4:]
    r_out, k_out, v_out, ld_out, a_out, g_out = rest
    i = pl.program_id(1)
    nw, sh, sc = nw_ref[...], sh_ref[0], sc_ref[0]

    def prep(xv):
        return _rms(xv, nw) * (1.0 + sc) + sh

    h = prep(x_ref[0])
    halo = prep(xh_ref[0])
    prev_last = jnp.where(i > 0, halo[7:8, :], 0.0)
    row = lax.broadcasted_iota(jnp.int32, h.shape, 0)
    hprev = jnp.where(row == 0, prev_last, pltpu.roll(h, 1, axis=0))
    xx = hprev - h

    def mix(j):
        return (h + xx * mu_ref[j:j + 1, :]).astype(BF16)

    xr, xw, xk, xv, xa, xg = (mix(j) for j in range(6))
    r_out[0] = _dot(xr, wrkv_ref[0])
    k_out[0] = _dot(xk, wrkv_ref[1])
    v = _dot(xv, wrkv_ref[2])
    if has_vres:
        gate = _sigmoid(v0_ref[...] + _dot(_dot(xv, v1_ref[...]).astype(BF16), v2_ref[...]))
        v = v + (vf_ref[0] - v) * gate
    v_out[0] = v
    wl = w0_ref[...] + _dot(jnp.tanh(_dot(xw, w1_ref[...])).astype(BF16), w2_ref[...])
    ld_out[0] = -jnp.exp(-_softplus(-wl) - 0.5)
    a_out[0] = _sigmoid(a0_ref[...] + _dot(_dot(xa, a1_ref[...]).astype(BF16), a2_ref[...]))
    g_out[0] = _dot(_sigmoid(_dot(xg, g1_ref[...])).astype(BF16), g2_ref[...]).astype(g_out.dtype)


def _rwkv_proj(x, nw, sh, sc, mu, wrkv, w0, w1, w2, a0, a1, a2, g1, g2, vres, ts):
    bsz, seq, d = x.shape
    has_vres = vres is not None
    row = lambda a: a.reshape(1, -1)
    full = lambda a: pl.BlockSpec(a.shape, lambda b, i: (0,) * a.ndim, pipeline_mode=pl.Buffered(1))
    tile = pl.BlockSpec((1, ts, d), lambda b, i: (b, i, 0))
    halo = pl.BlockSpec((1, 8, d), lambda b, i: (b, jnp.maximum(i * (ts // 8) - 1, 0), 0))
    per_b = pl.BlockSpec((1, 1, d), lambda b, i: (b, 0, 0))
    args = [x, x, row(nw), sh, sc, mu, wrkv, row(w0), w1, w2, row(a0), a1, a2, g1, g2]
    specs = [tile, halo, full(row(nw)), per_b, per_b, full(mu), full(wrkv), full(row(w0)), full(w1),
             full(w2), full(row(a0)), full(a1), full(a2), full(g1), full(g2)]
    if has_vres:
        v0, v1, v2, vf = vres
        args += [row(v0), v1, v2, vf]
        specs += [full(row(v0)), full(v1), full(v2), tile]
    out = jax.ShapeDtypeStruct((bsz, seq, d), F32)
    return pl.pallas_call(
        functools.partial(_rwkv_proj_kernel, has_vres),
        grid=(bsz, seq // ts),
        in_specs=specs,
        out_specs=[tile] * 6,
        out_shape=[out] * 5 + [jax.ShapeDtypeStruct((bsz, seq, d), BF16)],
        compiler_params=_cparams("parallel", "arbitrary"),
        name="rwkv_proj",
    )(*args)


def _wkv_consts(c, w):
    heads = w // HEAD
    ti = lax.broadcasted_iota(jnp.int32, (c, 3 * c), 0)
    tj = lax.broadcasted_iota(jnp.int32, (c, 3 * c), 1) % c
    t = lax.broadcasted_iota(jnp.int32, (c, 2 * heads * c), 0)
    s = lax.broadcasted_iota(jnp.int32, (c, 2 * heads * c), 1) % c
    vi = lax.broadcasted_iota(jnp.int32, (w, w), 0) // HEAD
    ki = lax.broadcasted_iota(jnp.int32, (w, w), 1) // HEAD
    return dict(
        tril3=(ti >= tj).astype(BF16),
        lane_head=lax.broadcasted_iota(jnp.int32, (c, w), 1) // HEAD,
        col_head=lax.broadcasted_iota(jnp.int32, (c, heads * c), 1) // c,
        strict=s < t, incl=s <= t, same_head=vi == ki)


def _wkv_chunk(inputs, params, states, consts):
    c, w = inputs[0][0].shape
    heads = w // HEAD
    rows = heads * c
    lane_head, col_head = consts["lane_head"], consts["col_head"]
    strict, incl, same_head = consts["strict"], consts["incl"], consts["same_head"]

    def stack(x):
        return jnp.concatenate([jnp.where(lane_head == hd, x, 0.0) for hd in range(heads)], axis=0)

    def segsum(x):
        out = jnp.sum(jnp.where(lane_head == 0, x, 0.0), axis=1, keepdims=True)
        for hd in range(1, heads):
            out = jnp.where(lane_head == hd, jnp.sum(jnp.where(lane_head == hd, x, 0.0), axis=1, keepdims=True), out)
        return out

    def cumulative(ld):
        ld_hi = ld.astype(BF16)
        rem = ld - ld_hi.astype(F32)
        ld_mid = rem.astype(BF16)
        ld_lo = (rem - ld_mid.astype(F32)).astype(BF16)
        return _dot(consts["tril3"], jnp.concatenate([ld_hi, ld_mid, ld_lo], axis=0))

    def prepare(inp, par, cum):
        r, k, v, ld, a, _ = inp
        k_k, k_a = par[0], par[1]
        last = cum[c - 1:c, :]
        g_in, g_ex, g_inv, g_end = jnp.exp(cum), jnp.exp(cum - ld), jnp.exp(-cum), jnp.exp(last - cum)
        kk = k * k_k
        kk = kk / jnp.maximum(jnp.sqrt(segsum(kk * kk)), 1e-12)
        k2 = k * (1.0 + (a - 1.0) * k_a)
        b_in = kk * a
        return dict(
            k2=k2, decay=jnp.exp(last), vs=stack(v).astype(BF16),
            lhs=jnp.concatenate([-kk * g_ex, r * g_in], axis=0).astype(BF16),
            rhs=jnp.concatenate([stack(b_in * g_inv), stack(k2 * g_inv)], axis=0).astype(BF16),
            bk_end=jnp.concatenate([b_in * g_end, k2 * g_end], axis=0).astype(BF16))

    cums = [cumulative(inp[3]) for inp in inputs]
    pre = [prepare(inp, par, cum) for inp, par, cum in zip(inputs, params, cums)]
    yield None
    aas = [_dot_nt(p["lhs"], p["rhs"]) for p in pre]
    pss = [_dot_nt(p["lhs"], st.astype(BF16)) for p, st in zip(pre, states)]
    a_aks = [jnp.where(strict[:, :rows], aa[:c, rows:], 0.0).astype(BF16) for aa in aas]
    a_rs = [jnp.where(incl, aa[c:, :], 0.0).astype(BF16) for aa in aas]

    def block_diag(aa):
        n = jnp.where(strict[:, :rows], aa[:c, :rows], 0.0)
        return jnp.concatenate([jnp.where(col_head == hd, n, 0.0) for hd in range(heads)], axis=0).astype(BF16)

    ns = [block_diag(aa) for aa in aas]
    us = [stack(ps[:c] + _dot(a_ak, p["vs"])) for ps, a_ak, p in zip(pss, a_aks, pre)]
    yield None
    steps = int(math.log2(c))
    for i in range(steps):
        if i < steps - 1:
            res = [_dot(n, jnp.concatenate([u.astype(BF16), n], axis=1)) for n, u in zip(ns, us)]
            us = [u + rs[:, :w] for u, rs in zip(us, res)]
            ns = [rs[:, w:].astype(BF16) for rs in res]
            yield None
        else:
            us = [u + _dot(n, u.astype(BF16)) for n, u in zip(ns, us)]

    uvs = [jnp.concatenate([u.astype(BF16), p["vs"]], axis=0) for u, p in zip(us, pre)]
    ys = [ps[c:] + _dot(a_r, uv) for ps, a_r, uv in zip(pss, a_rs, uvs)]

    def next_state(inp, st, p, u):
        u_tok = u[:c]
        for hd in range(1, heads):
            u_tok = u_tok + u[hd * c:(hd + 1) * c]
        uv_tok = jnp.concatenate([u_tok, inp[2]], axis=0).astype(BF16)
        return st * p["decay"] + jnp.where(same_head, _dot_tn(uv_tok, p["bk_end"]), 0.0)

    new_states = [next_state(inp, st, p, u) for inp, st, p, u in zip(inputs, states, pre, us)]
    yield None

    def finish(inp, par, p, y):
        r, _, v, _, _, g = inp
        r_k, ln_w, ln_b = par[2], par[3], par[4]
        mean = segsum(y) * (1.0 / HEAD)
        dlt = y - mean
        var = segsum(dlt * dlt) * (1.0 / HEAD)
        out = dlt * lax.rsqrt(var + GN_EPS) * ln_w + ln_b + segsum(r * p["k2"] * r_k) * v
        return out * g

    outs = [finish(inp, par, p, y) for inp, par, p, y in zip(inputs, params, pre, ys)]
    yield outs, new_states


WKV_STAGES = int(math.log2(WKV_CHUNK)) + 3
WKV_SEQS = 2
WKV_STAGE_LAG = 1


def _wkv_kernel(groups, r_ref, k_ref, v_ref, ld_ref, a_ref, g_ref, kk_ref, ka_ref, rk_ref, lnw_ref, lnb_ref,
                o_ref, s_ref):
    @pl.when(pl.program_id(2) == 0)
    def _():
        s_ref[...] = jnp.zeros_like(s_ref)

    n_chunks = r_ref.shape[1] // WKV_CHUNK
    consts = _wkv_consts(WKV_CHUNK, LANES)
    lanes = [pl.ds(gi * LANES, LANES) for gi in range(groups)]
    seqs = range(r_ref.shape[0])

    def body(ci, carry):
        tok = pl.ds(pl.multiple_of(ci * WKV_CHUNK, WKV_CHUNK), WKV_CHUNK)
        params = [(kk_ref[:, ln], ka_ref[:, ln], rk_ref[:, ln], lnw_ref[:, ln], lnb_ref[:, ln]) for ln in lanes]
        chunks = [_wkv_chunk(
            [(r_ref[sq, tok, ln], k_ref[sq, tok, ln], v_ref[sq, tok, ln], ld_ref[sq, tok, ln], a_ref[sq, tok, ln],
              g_ref[sq, tok, ln].astype(F32)) for ln in lanes],
            params, [s_ref[sq * groups + gi] for gi in range(groups)], consts) for sq in seqs]
        results = [None for _ in seqs]
        for tick in range(WKV_STAGES + WKV_STAGE_LAG * (len(seqs) - 1)):
            for sq in seqs:
                if 0 <= tick - WKV_STAGE_LAG * sq < WKV_STAGES:
                    results[sq] = next(chunks[sq])
        for sq in seqs:
            outs, new_states = results[sq]
            for gi, ln in enumerate(lanes):
                s_ref[sq * groups + gi] = new_states[gi]
                o_ref[sq, tok, ln] = outs[gi].astype(o_ref.dtype)
        return carry

    lax.fori_loop(0, n_chunks, body, 0)


def _wkv(r, k, v, ld, a, g, k_k, k_a, r_k, ln_w, ln_b, ts, groups=8):
    bsz, seq, d = r.shape
    wid = groups * LANES
    assert bsz % WKV_SEQS == 0
    tile = pl.BlockSpec((WKV_SEQS, ts, wid), lambda b, j, i: (b, i, j))
    par = pl.BlockSpec((1, wid), lambda b, j, i: (0, j))
    row = lambda t: t.reshape(1, d)
    return pl.pallas_call(
        functools.partial(_wkv_kernel, groups),
        grid=(bsz // WKV_SEQS, d // wid, seq // ts),
        in_specs=[tile] * 6 + [par] * 5,
        out_specs=tile,
        out_shape=jax.ShapeDtypeStruct((bsz, seq, d), BF16),
        scratch_shapes=[pltpu.VMEM((WKV_SEQS * groups, LANES, LANES), F32)],
        compiler_params=_cparams("parallel", "parallel", "arbitrary"),
        name="wkv7",
    )(r, k, v, ld, a, g, row(k_k), row(k_a), row(r_k), row(ln_w), row(ln_b))


FFN_HALO = 16
FFN_CHUNK = 256


def _ffn_kernel(final, x_ref, xh_ref, y_ref, yh_ref, wo_ref, g1_ref, nw_ref, sh_ref, sc_ref, g2_ref, wup_ref, cw_ref,
                cb_ref, wd_ref, *rest):
    if final:
        fn_ref, o_ref, h_buf, u_buf, act_buf = rest
    else:
        o_ref, h_buf, u_buf, act_buf = rest
    i = pl.program_id(1)
    ts = x_ref.shape[1]
    fc = FFN_CHUNK
    nw, sh, sc, g1 = nw_ref[...], sh_ref[0], sc_ref[0], g1_ref[0]

    x1 = x_ref[0] + g1 * _dot(y_ref[0], wo_ref[...])
    x1h = xh_ref[0] + g1 * _dot(yh_ref[0], wo_ref[...])
    h_buf[0:FFN_HALO, :] = jnp.where(i > 0, _rms(x1h, nw) * (1.0 + sc) + sh, 0.0).astype(BF16)
    h_buf[FFN_HALO:, :] = (_rms(x1, nw) * (1.0 + sc) + sh).astype(BF16)

    ff = wd_ref.shape[0]
    for c in range(ff // fc):
        slot = c % 2
        halves = []
        for cols, lanes in ((slice(c * fc, (c + 1) * fc), slice(0, fc)),
                            (slice(ff + c * fc, ff + (c + 1) * fc), slice(fc, 2 * fc))):
            u_buf[slot, :, lanes] = _dot(h_buf[...], wup_ref[:, cols])
            halves.append(cb_ref[:, cols] + cw_ref[0:1, cols] * u_buf[slot, FFN_HALO - 2:FFN_HALO - 2 + ts, lanes]
                          + cw_ref[1:2, cols] * u_buf[slot, FFN_HALO - 1:FFN_HALO - 1 + ts, lanes]
                          + cw_ref[2:3, cols] * u_buf[slot, FFN_HALO:, lanes])
        gate, val = halves
        act_buf[:, c * fc:(c + 1) * fc] = (gate * _sigmoid(gate) * val).astype(BF16)

    out = x1 + g2_ref[0] * _dot(act_buf[...], wd_ref[...])
    if final:
        out = _rms(out, fn_ref[...])
    o_ref[0] = out


def _ffn(x, y, w_o, g1, nw, sh, sc, g2, w_up, conv_w, conv_b, w_down, final_nw, ts):
    bsz, seq, d = x.shape
    ff = w_down.shape[0]
    assert ff % FFN_CHUNK == 0 and ff % LANES == 0
    final = final_nw is not None
    tile = lambda a: pl.BlockSpec((1, ts, a.shape[2]), lambda b, i: (b, i, 0))
    halo = lambda a: pl.BlockSpec((1, FFN_HALO, a.shape[2]),
                                  lambda b, i: (b, jnp.maximum(i * (ts // FFN_HALO) - 1, 0), 0))
    per_b = pl.BlockSpec((1, 1, d), lambda b, i: (b, 0, 0))
    once = lambda a: pl.BlockSpec(a.shape, lambda b, i: (0,) * a.ndim, pipeline_mode=pl.Buffered(1))
    nw = nw.reshape(1, d)
    conv_b = conv_b.reshape(1, -1)
    args = [x, x, y, y, w_o, g1, nw, sh, sc, g2, w_up, conv_w, conv_b, w_down]
    specs = [tile(x), halo(x), tile(y), halo(y), once(w_o), per_b, once(nw), per_b, per_b, per_b, once(w_up),
             once(conv_w), once(conv_b), once(w_down)]
    if final:
        args.append(final_nw.reshape(1, d))
        specs.append(once(args[-1]))
    return pl.pallas_call(
        functools.partial(_ffn_kernel, final),
        grid=(bsz, seq // ts),
        in_specs=specs,
        out_specs=tile(x),
        out_shape=jax.ShapeDtypeStruct(x.shape, F32),
        scratch_shapes=[pltpu.VMEM((ts + FFN_HALO, d), BF16), pltpu.VMEM((2, ts + FFN_HALO, 2 * FFN_CHUNK), F32),
                        pltpu.VMEM((ts, ff), BF16)],
        compiler_params=_cparams("parallel", "parallel"),
        name="proj_conv_glu_ffn",
    )(*args)


def _rope_tables(seq):
    half = ROT_DIM // 2
    pos = jnp.arange(seq, dtype=F32)
    inv = ROPE_THETA ** (-jnp.arange(0, ROT_DIM, 2, dtype=F32) / ROT_DIM)
    ang = pos[:, None] * inv[None, :]
    cos, sin = jnp.cos(ang), jnp.sin(ang)
    ones = jnp.ones((seq, HEAD - ROT_DIM), F32)
    zeros = jnp.zeros((seq, HEAD - half), F32)
    cos_t = jnp.concatenate([cos, cos, ones], axis=1)
    sin_lo = jnp.concatenate([-sin, zeros], axis=1)
    sin_hi = jnp.concatenate([jnp.zeros((seq, half), F32), sin, zeros[:, half:]], axis=1)
    rep = lambda t: jnp.concatenate([t] * (LANES // HEAD), axis=1)
    return rep(cos_t), rep(sin_lo), rep(sin_hi)


def _rope(t, cos_t, sin_lo, sin_hi):
    half = ROT_DIM // 2
    outs = []
    for j in range(t.shape[1] // LANES):
        tj = t[:, j * LANES:(j + 1) * LANES]
        outs.append(tj * cos_t + pltpu.roll(tj, LANES - half, axis=1) * sin_lo + pltpu.roll(tj, half, axis=1) * sin_hi)
    return jnp.concatenate(outs, axis=1)


def _q_proj_kernel(x_ref, nw_ref, sh_ref, sc_ref, w_ref, cos_ref, sl_ref, sh2_ref, q_out):
    h = (_rms(x_ref[0], nw_ref[...]) * (1.0 + sc_ref[0]) + sh_ref[0]).astype(BF16)
    q = _rope(_dot(h, w_ref[...]), cos_ref[...], sl_ref[...], sh2_ref[...])
    q_out[0] = (q * (HEAD ** -0.5 * LOG2_E)).astype(q_out.dtype)


def _kv_proj_kernel(x_ref, nw_ref, w_ref, cos_ref, sl_ref, sh2_ref, k_out, v_out):
    h = _rms(x_ref[0], nw_ref[...]).astype(BF16)
    kv = _dot(h, w_ref[...])
    nk = k_out.shape[2]
    k_out[0] = _rope(kv[:, :nk], cos_ref[...], sl_ref[...], sh2_ref[...]).astype(k_out.dtype)
    v_out[0] = kv[:, nk:].astype(v_out.dtype)


def _q_proj(x, nw, sh, sc, w, tables, ts):
    bsz, seq, d = x.shape
    tile = pl.BlockSpec((1, ts, d), lambda b, i: (b, i, 0))
    per_b = pl.BlockSpec((1, 1, d), lambda b, i: (b, 0, 0))
    tab = pl.BlockSpec((ts, LANES), lambda b, i: (i, 0))
    return pl.pallas_call(
        _q_proj_kernel,
        grid=(bsz, seq // ts),
        in_specs=[tile, pl.BlockSpec((1, d), lambda b, i: (0, 0)), per_b, per_b,
                  pl.BlockSpec(w.shape, lambda b, i: (0, 0)), tab, tab, tab],
        out_specs=pl.BlockSpec((1, ts, w.shape[1]), lambda b, i: (b, i, 0)),
        out_shape=jax.ShapeDtypeStruct((bsz, seq, w.shape[1]), BF16),
        compiler_params=_cparams("parallel", "parallel"),
        name="q_proj",
    )(x, nw.reshape(1, d), sh, sc, w, *tables)


def _kv_proj(x, nw, w, qk_width, tables, ts):
    bsz, seq, d = x.shape
    v_width = w.shape[1] - qk_width
    tile = pl.BlockSpec((1, ts, d), lambda b, i: (b, i, 0))
    tab = pl.BlockSpec((ts, LANES), lambda b, i: (i, 0))
    return pl.pallas_call(
        _kv_proj_kernel,
        grid=(bsz, seq // ts),
        in_specs=[tile, pl.BlockSpec((1, d), lambda b, i: (0, 0)), pl.BlockSpec(w.shape, lambda b, i: (0, 0)),
                  tab, tab, tab],
        out_specs=[pl.BlockSpec((1, ts, qk_width), lambda b, i: (b, i, 0)),
                   pl.BlockSpec((1, ts, v_width), lambda b, i: (b, i, 0))],
        out_shape=[jax.ShapeDtypeStruct((bsz, seq, qk_width), BF16),
                   jax.ShapeDtypeStruct((bsz, seq, v_width), BF16)],
        compiler_params=_cparams("parallel", "parallel"),
        name="kv_proj",
    )(x, nw.reshape(1, d), w, *tables)


MASKED = -1e30


def _attn_kernel(lam_init, q_ref, k_ref, v_ref, lam_ref, sub_ref, o_ref):
    qi = pl.program_id(2)
    tq = q_ref.shape[1]
    q = q_ref[0]
    lane = lax.broadcasted_iota(jnp.int32, q.shape, 1)
    zero = jnp.zeros_like(q)
    qq = jnp.concatenate([jnp.where(lane < HEAD, q, zero), jnp.where(lane >= HEAD, q, zero)], axis=0)

    def step(j, carry, diagonal):
        m, l, acc = carry
        tok = pl.ds(pl.multiple_of(j * tq, tq), tq)
        s = _dot_nt(qq, k_ref[0, tok, :]).astype(BF16)
        if diagonal:
            qpos = lax.broadcasted_iota(jnp.int32, s.shape, 0) % tq
            kpos = lax.broadcasted_iota(jnp.int32, s.shape, 1)
            s = jnp.where(kpos <= qpos, s, MASKED)
        m_new = jnp.maximum(m, jnp.max(s, axis=1, keepdims=True))
        p = jnp.exp2(s - m_new)
        alpha = jnp.exp2((m - m_new).astype(F32))
        l = alpha * l + jnp.sum(p.astype(F32), axis=1, keepdims=True)
        acc = alpha * acc + _dot(p, v_ref[0, tok, :])
        return m_new, l, acc

    init = (jnp.full((2 * tq, 1), MASKED, BF16), jnp.zeros((2 * tq, 1), F32), jnp.zeros((2 * tq, 2 * HEAD), F32))
    carry = lax.fori_loop(0, qi, lambda j, c: step(j, c, False), init)
    _, l, acc = step(qi, carry, True)
    o = acc / l
    lv = lam_ref[...]
    lam = (jnp.exp(jnp.sum(lv[0:1] * lv[1:2], axis=1, keepdims=True))
           - jnp.exp(jnp.sum(lv[2:3] * lv[3:4], axis=1, keepdims=True)) + lam_init)
    dlt = o[:tq] - lam * o[tq:]
    o_ref[0] = (_rms(dlt, sub_ref[...]) * (1.0 - lam_init)).astype(o_ref.dtype)


def _diff_attn(q, k, v, lam_vecs, subln, lam_init, tq):
    bsz, seq, d = q.shape
    wid = 2 * HEAD
    return pl.pallas_call(
        functools.partial(_attn_kernel, lam_init),
        grid=(bsz, d // wid, seq // tq),
        in_specs=[pl.BlockSpec((1, tq, wid), lambda b, h, i: (b, i, h)),
                  pl.BlockSpec((1, seq, wid), lambda b, h, i: (b, 0, h)),
                  pl.BlockSpec((1, seq, wid), lambda b, h, i: (b, 0, h)),
                  pl.BlockSpec(lam_vecs.shape, lambda b, h, i: (0, 0)),
                  pl.BlockSpec((1, wid), lambda b, h, i: (0, 0))],
        out_specs=pl.BlockSpec((1, tq, wid), lambda b, h, i: (b, i, h)),
        out_shape=jax.ShapeDtypeStruct((bsz, seq, d), BF16),
        compiler_params=_cparams("parallel", "parallel", "arbitrary"),
        name="diff_attn",
    )(q, k, v, lam_vecs, subln.reshape(1, wid))


def kernel(x, c, ada_w, ada_b, norm1, norm2, final_norm, a_mu, a_w_rkv, a_w0, a_w1, a_w2, a_a0, a_a1, a_a2, a_v0, a_v1, a_v2, a_g1, a_g2, a_k_k, a_k_a, a_r_k, a_ln_w, a_ln_b, a_w_o, kv_norm, w_kv, b_w_q, b_lam, b_subln, b_w_o, ffn_w_up, ffn_conv_w, ffn_conv_b, ffn_w_down):
    bsz, seq, d = x.shape
    depth = ada_w.shape[0]
    n_a = a_mu.shape[0]
    qk_width = b_w_q.shape[2]
    t = _tiles(seq)
    bf = lambda w: w.astype(BF16)
    tables = _rope_tables(seq)

    mod = _ada_mod(c, ada_w, ada_b)
    v_first = None
    k_sh = v_sh = None
    for l in range(depth):
        sh1, sc1, g1, sh2, sc2, g2 = (mod[l, :, j * d:(j + 1) * d].reshape(bsz, 1, d) for j in range(6))
        if l < n_a:
            vres = None if l == 0 else (a_v0[l - 1], bf(a_v1[l - 1]), bf(a_v2[l - 1]), v_first)
            r, k, v, ld, a, g = _rwkv_proj(
                x, norm1[l], sh1, sc1, a_mu[l], bf(a_w_rkv[l]), a_w0[l], bf(a_w1[l]), bf(a_w2[l]), a_a0[l],
                bf(a_a1[l]), bf(a_a2[l]), bf(a_g1[l]), bf(a_g2[l]), vres, t["proj"])
            if l == 0:
                v_first = v
            y = _wkv(r, k, v, ld, a, g, a_k_k[l], a_k_a[l], a_r_k[l], a_ln_w[l], a_ln_b[l], t["wkv"])
            w_o = a_w_o[l]
        else:
            j = l - n_a
            if j == 0:
                k_sh, v_sh = _kv_proj(x, kv_norm, bf(w_kv), qk_width, tables, t["qkv"])
            lam_init = 0.8 - 0.6 * math.exp(-0.3 * l)
            q = _q_proj(x, norm1[l], sh1, sc1, bf(b_w_q[j]), tables, t["qkv"])
            y = _diff_attn(q, k_sh, v_sh, b_lam[j], b_subln[j], lam_init, t["attn"])
            w_o = b_w_o[j]
        x = _ffn(x, y, bf(w_o), g1, norm2[l], sh2, sc2, g2, bf(ffn_w_up[l]), ffn_conv_w[l], ffn_conv_b[l],
                 bf(ffn_w_down[l]), final_norm if l == depth - 1 else None, t["ffn"])
    return x
```

```python
import functools
import math

import jax
import jax.numpy as jnp
from jax import lax
from jax.experimental import pallas as pl
from jax.experimental.pallas import tpu as pltpu

F32 = jnp.float32
BF16 = jnp.bfloat16
HIGHEST = lax.Precision.HIGHEST

NORM_EPS = 1e-6
GN_EPS = 64e-5
ROPE_THETA = 500000.0
LOG2_E = math.log2(math.e)
HEAD = 64
ROT_DIM = HEAD // 4
LANES = 128
WKV_CHUNK = 64
WKV_HEADS_PER_GROUP = LANES // HEAD
VMEM_LIMIT = 56 * 1024 * 1024


def _tiles(seq):
    def fit(t):
        t = min(t, seq)
        assert seq % t == 0
        return t
    return dict(proj=fit(512), wkv=fit(256), ffn=fit(512), qkv=fit(512), attn=fit(512))


def _cparams(*sem):
    return pltpu.CompilerParams(dimension_semantics=sem, vmem_limit_bytes=VMEM_LIMIT)


def _dot(a, b, **kw):
    return jnp.dot(a, b, preferred_element_type=F32, **kw)


def _dot_nt(a, b):
    return lax.dot_general(a, b, (((1,), (1,)), ((), ())), preferred_element_type=F32)


def _dot_tn(a, b):
    return lax.dot_general(a, b, (((0,), (0,)), ((), ())), preferred_element_type=F32)


def _rms(x, g):
    return x * lax.rsqrt(jnp.mean(x * x, axis=-1, keepdims=True) + NORM_EPS) * g


def _sigmoid(x):
    return 1.0 / (1.0 + jnp.exp(-x))


def _mod_kernel(c_ref, w_ref, b_ref, o_ref):
    c = c_ref[...]
    o_ref[0] = _dot(c * _sigmoid(c), w_ref[0], precision=HIGHEST) + b_ref[0]


def _ada_mod(c, ada_w, ada_b):
    depth, d, n = ada_w.shape
    bsz = c.shape[0]
    tn = d
    return pl.pallas_call(
        _mod_kernel,
        grid=(depth, n // tn),
        in_specs=[pl.BlockSpec((bsz, d), lambda l, j: (0, 0)),
                  pl.BlockSpec((1, d, tn), lambda l, j: (l, 0, j)),
                  pl.BlockSpec((1, 1, tn), lambda l, j: (l, 0, j))],
        out_specs=pl.BlockSpec((1, bsz, tn), lambda l, j: (l, 0, j)),
        out_shape=jax.ShapeDtypeStruct((depth, bsz, n), F32),
        compiler_params=_cparams("parallel", "parallel"),
        name="ada_mod",
    )(c, ada_w, ada_b.reshape(depth, 1, n))


def _rwkv_proj_kernel(has_vres, x_ref, xh_ref, nw_ref, sh_ref, sc_ref, mu_ref, wrkv_ref,
                      w0_ref, w1_ref, w2_ref, a0_ref, a1_ref, a2_ref, g1_ref, g2_ref, *rest):
    if has_vres:
        v0_ref, v1_ref, v2_ref, vf_ref = rest[:4]
        rest = rest[4:]
    r_out, k_out, v_out, ld_out, a_out, g_out = rest
    i = pl.program_id(1)
    nw, sh, sc = nw_ref[...], sh_ref[0], sc_ref[0]

    def prep(xv):
        return _rms(xv, nw) * (1.0 + sc) + sh

    h = prep(x_ref[0])
    halo = prep(xh_ref[0])
    prev_last = jnp.where(i > 0, halo[7:8, :], 0.0)
    row = lax.broadcasted_iota(jnp.int32, h.shape, 0)
    hprev = jnp.where(row == 0, prev_last, pltpu.roll(h, 1, axis=0))
    xx = hprev - h

    def mix(j):
        return (h + xx * mu_ref[j:j + 1, :]).astype(BF16)

    xr, xw, xk, xv, xa, xg = (mix(j) for j in range(6))
    r_out[0] = _dot(xr, wrkv_ref[0])
    k_out[0] = _dot(xk, wrkv_ref[1])
    v = _dot(xv, wrkv_ref[2])
    if has_vres:
        gate = _sigmoid(v0_ref[...] + _dot(_dot(xv, v1_ref[...]).astype(BF16), v2_ref[...]))
        v = v + (vf_ref[0] - v) * gate
    v_out[0] = v
    wl = w0_ref[...] + _dot(jnp.tanh(_dot(xw, w1_ref[...])).astype(BF16), w2_ref[...])
    ld_out[0] = -math.exp(-0.5) * _sigmoid(wl)
    a_out[0] = _sigmoid(a0_ref[...] + _dot(_dot(xa, a1_ref[...]).astype(BF16), a2_ref[...]))
    g_out[0] = _dot(_sigmoid(_dot(xg, g1_ref[...])).astype(BF16), g2_ref[...]).astype(g_out.dtype)


def _rwkv_proj(x, nw, sh, sc, mu, wrkv, w0, w1, w2, a0, a1, a2, g1, g2, vres, ts):
    bsz, seq, d = x.shape
    has_vres = vres is not None
    row = lambda a: a.reshape(1, -1)
    full = lambda a: pl.BlockSpec(a.shape, lambda b, i: (0,) * a.ndim, pipeline_mode=pl.Buffered(1))
    tile = pl.BlockSpec((1, ts, d), lambda b, i: (b, i, 0))
    halo = pl.BlockSpec((1, 8, d), lambda b, i: (b, jnp.maximum(i * (ts // 8) - 1, 0), 0))
    per_b = pl.BlockSpec((1, 1, d), lambda b, i: (b, 0, 0))
    args = [x, x, row(nw), sh, sc, mu, wrkv, row(w0), w1, w2, row(a0), a1, a2, g1, g2]
    specs = [tile, halo, full(row(nw)), per_b, per_b, full(mu), full(wrkv), full(row(w0)), full(w1),
             full(w2), full(row(a0)), full(a1), full(a2), full(g1), full(g2)]
    if has_vres:
        v0, v1, v2, vf = vres
        args += [row(v0), v1, v2, vf]
        specs += [full(row(v0)), full(v1), full(v2), tile]
    out = jax.ShapeDtypeStruct((bsz, seq, d), F32)
    return pl.pallas_call(
        functools.partial(_rwkv_proj_kernel, has_vres),
        grid=(bsz, seq // ts),
        in_specs=specs,
        out_specs=[tile] * 6,
        out_shape=[out] * 5 + [jax.ShapeDtypeStruct((bsz, seq, d), BF16)],
        compiler_params=_cparams("parallel", "arbitrary"),
        name="rwkv_proj",
    )(*args)


def _wkv_consts(c, w):
    heads = w // HEAD
    ti = lax.broadcasted_iota(jnp.int32, (c, 3 * c), 0)
    tj = lax.broadcasted_iota(jnp.int32, (c, 3 * c), 1) % c
    t = lax.broadcasted_iota(jnp.int32, (c, 2 * heads * c), 0)
    s = lax.broadcasted_iota(jnp.int32, (c, 2 * heads * c), 1) % c
    vi = lax.broadcasted_iota(jnp.int32, (w, w), 0) // HEAD
    ki = lax.broadcasted_iota(jnp.int32, (w, w), 1) // HEAD
    return dict(
        tril3=(ti >= tj).astype(BF16),
        lane_head=lax.broadcasted_iota(jnp.int32, (c, w), 1) // HEAD,
        col_head=lax.broadcasted_iota(jnp.int32, (c, heads * c), 1) // c,
        strict=s < t, incl=s <= t, same_head=vi == ki)


def _wkv_chunk(inputs, params, states, consts):
    c, w = inputs[0][0].shape
    heads = w // HEAD
    rows = heads * c
    lane_head, col_head = consts["lane_head"], consts["col_head"]
    strict, incl, same_head = consts["strict"], consts["incl"], consts["same_head"]

    def stack(x):
        return jnp.concatenate([jnp.where(lane_head == hd, x, 0.0) for hd in range(heads)], axis=0)

    def segsum(x):
        out = jnp.sum(jnp.where(lane_head == 0, x, 0.0), axis=1, keepdims=True)
        for hd in range(1, heads):
            out = jnp.where(lane_head == hd, jnp.sum(jnp.where(lane_head == hd, x, 0.0), axis=1, keepdims=True), out)
        return out

    def cumulative(ld):
        ld_hi = ld.astype(BF16)
        rem = ld - ld_hi.astype(F32)
        ld_mid = rem.astype(BF16)
        ld_lo = (rem - ld_mid.astype(F32)).astype(BF16)
        return _dot(consts["tril3"], jnp.concatenate([ld_hi, ld_mid, ld_lo], axis=0))

    def prepare(inp, par, cum):
        r, k, v, ld, a, _ = inp
        k_k, k_a = par[0], par[1]
        last = cum[c - 1:c, :]
        g_in, g_ex, g_inv, g_end = jnp.exp(cum), jnp.exp(cum - ld), jnp.exp(-cum), jnp.exp(last - cum)
        kk = k * k_k
        kk = kk / jnp.maximum(jnp.sqrt(segsum(kk * kk)), 1e-12)
        k2 = k * (1.0 + (a - 1.0) * k_a)
        b_in = kk * a
        return dict(
            k2=k2, decay=jnp.exp(last), vs=stack(v).astype(BF16),
            lhs=jnp.concatenate([-kk * g_ex, r * g_in], axis=0).astype(BF16),
            rhs=jnp.concatenate([stack(b_in * g_inv), stack(k2 * g_inv)], axis=0).astype(BF16),
            bk_end=jnp.concatenate([b_in * g_end, k2 * g_end], axis=0).astype(BF16))

    cums = [cumulative(inp[3]) for inp in inputs]
    pre = [prepare(inp, par, cum) for inp, par, cum in zip(inputs, params, cums)]
    yield None
    aas = [_dot_nt(p["lhs"], p["rhs"]) for p in pre]
    pss = [_dot_nt(p["lhs"], st.astype(BF16)) for p, st in zip(pre, states)]
    a_aks = [jnp.where(strict[:, :rows], aa[:c, rows:], 0.0).astype(BF16) for aa in aas]
    a_rs = [jnp.where(incl, aa[c:, :], 0.0).astype(BF16) for aa in aas]

    def block_diag(aa):
        n = jnp.where(strict[:, :rows], aa[:c, :rows], 0.0)
        return jnp.concatenate([jnp.where(col_head == hd, n, 0.0) for hd in range(heads)], axis=0).astype(BF16)

    ns = [block_diag(aa) for aa in aas]
    us = [stack(ps[:c] + _dot(a_ak, p["vs"])) for ps, a_ak, p in zip(pss, a_aks, pre)]
    yield None
    steps = int(math.log2(c))
    for i in range(steps):
        if i < steps - 1:
            res = [_dot(n, jnp.concatenate([u.astype(BF16), n], axis=1)) for n, u in zip(ns, us)]
            us = [u + rs[:, :w] for u, rs in zip(us, res)]
            ns = [rs[:, w:].astype(BF16) for rs in res]
            yield None
        else:
            us = [u + _dot(n, u.astype(BF16)) for n, u in zip(ns, us)]

    uvs = [jnp.concatenate([u.astype(BF16), p["vs"]], axis=0) for u, p in zip(us, pre)]
    ys = [ps[c:] + _dot(a_r, uv) for ps, a_r, uv in zip(pss, a_rs, uvs)]

    def next_state(inp, st, p, u):
        u_tok = u[:c]
        for hd in range(1, heads):
            u_tok = u_tok + u[hd * c:(hd + 1) * c]
        uv_tok = jnp.concatenate([u_tok, inp[2]], axis=0).astype(BF16)
        return st * p["decay"] + jnp.where(same_head, _dot_tn(uv_tok, p["bk_end"]), 0.0)

    new_states = [next_state(inp, st, p, u) for inp, st, p, u in zip(inputs, states, pre, us)]
    yield None

    def finish(inp, par, p, y):
        r, _, v, _, _, g = inp
        r_k, ln_w, ln_b = par[2], par[3], par[4]
        mean = segsum(y) * (1.0 / HEAD)
        dlt = y - mean
        var = segsum(dlt * dlt) * (1.0 / HEAD)
        out = dlt * lax.rsqrt(var + GN_EPS) * ln_w + ln_b + segsum(r * p["k2"] * r_k) * v
        return out * g

    outs = [finish(inp, par, p, y) for inp, par, p, y in zip(inputs, params, pre, ys)]
    yield outs, new_states


WKV_STAGES = int(math.log2(WKV_CHUNK)) + 3
WKV_SEQS = 2
WKV_STAGE_LAG = 1


def _wkv_kernel(groups, r_ref, k_ref, v_ref, ld_ref, a_ref, g_ref, kk_ref, ka_ref, rk_ref, lnw_ref, lnb_ref,
                o_ref, s_ref):
    @pl.when(pl.program_id(2) == 0)
    def _():
        s_ref[...] = jnp.zeros_like(s_ref)

    n_chunks = r_ref.shape[1] // WKV_CHUNK
    consts = _wkv_consts(WKV_CHUNK, LANES)
    lanes = [pl.ds(gi * LANES, LANES) for gi in range(groups)]
    seqs = range(r_ref.shape[0])

    def body(ci, carry):
        tok = pl.ds(pl.multiple_of(ci * WKV_CHUNK, WKV_CHUNK), WKV_CHUNK)
        params = [(kk_ref[:, ln], ka_ref[:, ln], rk_ref[:, ln], lnw_ref[:, ln], lnb_ref[:, ln]) for ln in lanes]
        chunks = [_wkv_chunk(
            [(r_ref[sq, tok, ln], k_ref[sq, tok, ln], v_ref[sq, tok, ln], ld_ref[sq, tok, ln], a_ref[sq, tok, ln],
              g_ref[sq, tok, ln].astype(F32)) for ln in lanes],
            params, [s_ref[sq * groups + gi] for gi in range(groups)], consts) for sq in seqs]
        results = [None for _ in seqs]
        for tick in range(WKV_STAGES + WKV_STAGE_LAG * (len(seqs) - 1)):
            for sq in seqs:
                if 0 <= tick - WKV_STAGE_LAG * sq < WKV_STAGES:
                    results[sq] = next(chunks[sq])
        for sq in seqs:
            outs, new_states = results[sq]
            for gi, ln in enumerate(lanes):
                s_ref[sq * groups + gi] = new_states[gi]
                o_ref[sq, tok, ln] = outs[gi].astype(o_ref.dtype)
        return carry

    lax.fori_loop(0, n_chunks, body, 0)


def _wkv(r, k, v, ld, a, g, k_k, k_a, r_k, ln_w, ln_b, ts, groups=8):
    bsz, seq, d = r.shape
    wid = groups * LANES
    assert bsz % WKV_SEQS == 0
    tile = pl.BlockSpec((WKV_SEQS, ts, wid), lambda b, j, i: (b, i, j))
    par = pl.BlockSpec((1, wid), lambda b, j, i: (0, j))
    row = lambda t: t.reshape(1, d)
    return pl.pallas_call(
        functools.partial(_wkv_kernel, groups),
        grid=(bsz // WKV_SEQS, d // wid, seq // ts),
        in_specs=[tile] * 6 + [par] * 5,
        out_specs=tile,
        out_shape=jax.ShapeDtypeStruct((bsz, seq, d), BF16),
        scratch_shapes=[pltpu.VMEM((WKV_SEQS * groups, LANES, LANES), F32)],
        compiler_params=_cparams("parallel", "parallel", "arbitrary"),
        name="wkv7",
    )(r, k, v, ld, a, g, row(k_k), row(k_a), row(r_k), row(ln_w), row(ln_b))


FFN_HALO = 16
FFN_CHUNK = 256


def _ffn_kernel(final, x_ref, xh_ref, y_ref, yh_ref, wo_ref, g1_ref, nw_ref, sh_ref, sc_ref, g2_ref, wup_ref, cw_ref,
                cb_ref, wd_ref, *rest):
    if final:
        fn_ref, o_ref, h_buf, u_buf, act_buf = rest
    else:
        o_ref, h_buf, u_buf, act_buf = rest
    i = pl.program_id(1)
    ts = x_ref.shape[1]
    fc = FFN_CHUNK
    nw, sh, sc, g1 = nw_ref[...], sh_ref[0], sc_ref[0], g1_ref[0]

    x1 = x_ref[0] + g1 * _dot(y_ref[0], wo_ref[...])
    x1h = xh_ref[0] + g1 * _dot(yh_ref[0], wo_ref[...])
    h_buf[0:FFN_HALO, :] = jnp.where(i > 0, _rms(x1h, nw) * (1.0 + sc) + sh, 0.0).astype(BF16)
    h_buf[FFN_HALO:, :] = (_rms(x1, nw) * (1.0 + sc) + sh).astype(BF16)

    ff = wd_ref.shape[0]
    for c in range(ff // fc):
        slot = c % 2
        halves = []
        for cols, lanes in ((slice(c * fc, (c + 1) * fc), slice(0, fc)),
                            (slice(ff + c * fc, ff + (c + 1) * fc), slice(fc, 2 * fc))):
            u_buf[slot, :, lanes] = _dot(h_buf[...], wup_ref[:, cols])
            halves.append(cb_ref[:, cols] + cw_ref[0:1, cols] * u_buf[slot, FFN_HALO - 2:FFN_HALO - 2 + ts, lanes]
                          + cw_ref[1:2, cols] * u_buf[slot, FFN_HALO - 1:FFN_HALO - 1 + ts, lanes]
                          + cw_ref[2:3, cols] * u_buf[slot, FFN_HALO:, lanes])
        gate, val = halves
        act_buf[:, c * fc:(c + 1) * fc] = (gate * _sigmoid(gate) * val).astype(BF16)

    out = x1 + g2_ref[0] * _dot(act_buf[...], wd_ref[...])
    if final:
        out = _rms(out, fn_ref[...])
    o_ref[0] = out


def _ffn(x, y, w_o, g1, nw, sh, sc, g2, w_up, conv_w, conv_b, w_down, final_nw, ts):
    bsz, seq, d = x.shape
    ff = w_down.shape[0]
    assert ff % FFN_CHUNK == 0 and ff % LANES == 0
    final = final_nw is not None
    tile = lambda a: pl.BlockSpec((1, ts, a.shape[2]), lambda b, i: (b, i, 0))
    halo = lambda a: pl.BlockSpec((1, FFN_HALO, a.shape[2]),
                                  lambda b, i: (b, jnp.maximum(i * (ts // FFN_HALO) - 1, 0), 0))
    per_b = pl.BlockSpec((1, 1, d), lambda b, i: (b, 0, 0))
    once = lambda a: pl.BlockSpec(a.shape, lambda b, i: (0,) * a.ndim, pipeline_mode=pl.Buffered(1))
    nw = nw.reshape(1, d)
    conv_b = conv_b.reshape(1, -1)
    args = [x, x, y, y, w_o, g1, nw, sh, sc, g2, w_up, conv_w, conv_b, w_down]
    specs = [tile(x), halo(x), tile(y), halo(y), once(w_o), per_b, once(nw), per_b, per_b, per_b, once(w_up),
             once(conv_w), once(conv_b), once(w_down)]
    if final:
        args.append(final_nw.reshape(1, d))
        specs.append(once(args[-1]))
    return pl.pallas_call(
        functools.partial(_ffn_kernel, final),
        grid=(bsz, seq // ts),
        in_specs=specs,
        out_specs=tile(x),
        out_shape=jax.ShapeDtypeStruct(x.shape, F32),
        scratch_shapes=[pltpu.VMEM((ts + FFN_HALO, d), BF16), pltpu.VMEM((2, ts + FFN_HALO, 2 * FFN_CHUNK), F32),
                        pltpu.VMEM((ts, ff), BF16)],
        compiler_params=_cparams("parallel", "parallel"),
        name="proj_conv_glu_ffn",
    )(*args)


def _rope_tables(seq):
    half = ROT_DIM // 2
    pos = jnp.arange(seq, dtype=F32)
    inv = ROPE_THETA ** (-jnp.arange(0, ROT_DIM, 2, dtype=F32) / ROT_DIM)
    ang = pos[:, None] * inv[None, :]
    cos, sin = jnp.cos(ang), jnp.sin(ang)
    ones = jnp.ones((seq, HEAD - ROT_DIM), F32)
    zeros = jnp.zeros((seq, HEAD - half), F32)
    cos_t = jnp.concatenate([cos, cos, ones], axis=1)
    sin_lo = jnp.concatenate([-sin, zeros], axis=1)
    sin_hi = jnp.concatenate([jnp.zeros((seq, half), F32), sin, zeros[:, half:]], axis=1)
    rep = lambda t: jnp.concatenate([t] * (LANES // HEAD), axis=1)
    return rep(cos_t), rep(sin_lo), rep(sin_hi)


def _rope(t, cos_t, sin_lo, sin_hi):
    half = ROT_DIM // 2
    outs = []
    for j in range(t.shape[1] // LANES):
        tj = t[:, j * LANES:(j + 1) * LANES]
        outs.append(tj * cos_t + pltpu.roll(tj, LANES - half, axis=1) * sin_lo + pltpu.roll(tj, half, axis=1) * sin_hi)
    return jnp.concatenate(outs, axis=1)


def _q_proj_kernel(x_ref, nw_ref, sh_ref, sc_ref, w_ref, cos_ref, sl_ref, sh2_ref, q_out):
    h = (_rms(x_ref[0], nw_ref[...]) * (1.0 + sc_ref[0]) + sh_ref[0]).astype(BF16)
    q = _rope(_dot(h, w_ref[...]), cos_ref[...], sl_ref[...], sh2_ref[...])
    q_out[0] = (q * (HEAD ** -0.5 * LOG2_E)).astype(q_out.dtype)


def _kv_proj_kernel(x_ref, nw_ref, w_ref, cos_ref, sl_ref, sh2_ref, k_out, v_out):
    h = _rms(x_ref[0], nw_ref[...]).astype(BF16)
    kv = _dot(h, w_ref[...])
    nk = k_out.shape[2]
    k_out[0] = _rope(kv[:, :nk], cos_ref[...], sl_ref[...], sh2_ref[...]).astype(k_out.dtype)
    v_out[0] = kv[:, nk:].astype(v_out.dtype)


def _q_proj(x, nw, sh, sc, w, tables, ts):
    bsz, seq, d = x.shape
    tile = pl.BlockSpec((1, ts, d), lambda b, i: (b, i, 0))
    per_b = pl.BlockSpec((1, 1, d), lambda b, i: (b, 0, 0))
    tab = pl.BlockSpec((ts, LANES), lambda b, i: (i, 0))
    return pl.pallas_call(
        _q_proj_kernel,
        grid=(bsz, seq // ts),
        in_specs=[tile, pl.BlockSpec((1, d), lambda b, i: (0, 0)), per_b, per_b,
                  pl.BlockSpec(w.shape, lambda b, i: (0, 0)), tab, tab, tab],
        out_specs=pl.BlockSpec((1, ts, w.shape[1]), lambda b, i: (b, i, 0)),
        out_shape=jax.ShapeDtypeStruct((bsz, seq, w.shape[1]), BF16),
        compiler_params=_cparams("parallel", "parallel"),
        name="q_proj",
    )(x, nw.reshape(1, d), sh, sc, w, *tables)


def _kv_proj(x, nw, w, qk_width, tables, ts):
    bsz, seq, d = x.shape
    v_width = w.shape[1] - qk_width
    tile = pl.BlockSpec((1, ts, d), lambda b, i: (b, i, 0))
    tab = pl.BlockSpec((ts, LANES), lambda b, i: (i, 0))
    return pl.pallas_call(
        _kv_proj_kernel,
        grid=(bsz, seq // ts),
        in_specs=[tile, pl.BlockSpec((1, d), lambda b, i: (0, 0)), pl.BlockSpec(w.shape, lambda b, i: (0, 0)),
                  tab, tab, tab],
        out_specs=[pl.BlockSpec((1, ts, qk_width), lambda b, i: (b, i, 0)),
                   pl.BlockSpec((1, ts, v_width), lambda b, i: (b, i, 0))],
        out_shape=[jax.ShapeDtypeStruct((bsz, seq, qk_width), BF16),
                   jax.ShapeDtypeStruct((bsz, seq, v_width), BF16)],
        compiler_params=_cparams("parallel", "parallel"),
        name="kv_proj",
    )(x, nw.reshape(1, d), w, *tables)


MASKED = -1e30


def _attn_kernel(lam_init, q_ref, k_ref, v_ref, lam_ref, sub_ref, o_ref):
    qi = pl.program_id(2)
    tq = q_ref.shape[1]
    q = q_ref[0]
    lane = lax.broadcasted_iota(jnp.int32, q.shape, 1)
    zero = jnp.zeros_like(q)
    qq = jnp.concatenate([jnp.where(lane < HEAD, q, zero), jnp.where(lane >= HEAD, q, zero)], axis=0)

    def step(j, carry, diagonal):
        m, l, acc = carry
        tok = pl.ds(pl.multiple_of(j * tq, tq), tq)
        s = _dot_nt(qq, k_ref[0, tok, :]).astype(BF16)
        if diagonal:
            qpos = lax.broadcasted_iota(jnp.int32, s.shape, 0) % tq
            kpos = lax.broadcasted_iota(jnp.int32, s.shape, 1)
            s = jnp.where(kpos <= qpos, s, MASKED)
        m_new = jnp.maximum(m, jnp.max(s, axis=1, keepdims=True))
        p = jnp.exp2(s - m_new)
        alpha = jnp.exp2((m - m_new).astype(F32))
        l = alpha * l + jnp.sum(p.astype(F32), axis=1, keepdims=True)
        acc = alpha * acc + _dot(p, v_ref[0, tok, :])
        return m_new, l, acc

    init = (jnp.full((2 * tq, 1), MASKED, BF16), jnp.zeros((2 * tq, 1), F32), jnp.zeros((2 * tq, 2 * HEAD), F32))
    carry = lax.fori_loop(0, qi, lambda j, c: step(j, c, False), init)
    _, l, acc = step(qi, carry, True)
    o = acc / l
    lv = lam_ref[...]
    lam = (jnp.exp(jnp.sum(lv[0:1] * lv[1:2], axis=1, keepdims=True))
           - jnp.exp(jnp.sum(lv[2:3] * lv[3:4], axis=1, keepdims=True)) + lam_init)
    dlt = o[:tq] - lam * o[tq:]
    o_ref[0] = (_rms(dlt, sub_ref[...]) * (1.0 - lam_init)).astype(o_ref.dtype)


def _diff_attn(q, k, v, lam_vecs, subln, lam_init, tq):
    bsz, seq, d = q.shape
    wid = 2 * HEAD
    return pl.pallas_call(
        functools.partial(_attn_kernel, lam_init),
        grid=(bsz, d // wid, seq // tq),
        in_specs=[pl.BlockSpec((1, tq, wid), lambda b, h, i: (b, i, h)),
                  pl.BlockSpec((1, seq, wid), lambda b, h, i: (b, 0, h)),
                  pl.BlockSpec((1, seq, wid), lambda b, h, i: (b, 0, h)),
                  pl.BlockSpec(lam_vecs.shape, lambda b, h, i: (0, 0)),
                  pl.BlockSpec((1, wid), lambda b, h, i: (0, 0))],
        out_specs=pl.BlockSpec((1, tq, wid), lambda b, h, i: (b, i, h)),
        out_shape=jax.ShapeDtypeStruct((bsz, seq, d), BF16),
        compiler_params=_cparams("parallel", "parallel", "arbitrary"),
        name="diff_attn",
    )(q, k, v, lam_vecs, subln.reshape(1, wid))


def kernel(x, c, ada_w, ada_b, norm1, norm2, final_norm, a_mu, a_w_rkv, a_w0, a_w1, a_w2, a_a0, a_a1, a_a2, a_v0, a_v1, a_v2, a_g1, a_g2, a_k_k, a_k_a, a_r_k, a_ln_w, a_ln_b, a_w_o, kv_norm, w_kv, b_w_q, b_lam, b_subln, b_w_o, ffn_w_up, ffn_conv_w, ffn_conv_b, ffn_w_down):
    bsz, seq, d = x.shape
    depth = ada_w.shape[0]
    n_a = a_mu.shape[0]
    qk_width = b_w_q.shape[2]
    t = _tiles(seq)
    bf = lambda w: w.astype(BF16)
    tables = _rope_tables(seq)

    mod = _ada_mod(c, ada_w, ada_b)
    v_first = None
    k_sh = v_sh = None
    for l in range(depth):
        sh1, sc1, g1, sh2, sc2, g2 = (mod[l, :, j * d:(j + 1) * d].reshape(bsz, 1, d) for j in range(6))
        if l < n_a:
            vres = None if l == 0 else (a_v0[l - 1], bf(a_v1[l - 1]), bf(a_v2[l - 1]), v_first)
            r, k, v, ld, a, g = _rwkv_proj(
                x, norm1[l], sh1, sc1, a_mu[l], bf(a_w_rkv[l]), a_w0[l], bf(a_w1[l]), bf(a_w2[l]), a_a0[l],
                bf(a_a1[l]), bf(a_a2[l]), bf(a_g1[l]), bf(a_g2[l]), vres, t["proj"])
            if l == 0:
                v_first = v
            y = _wkv(r, k, v, ld, a, g, a_k_k[l], a_k_a[l], a_r_k[l], a_ln_w[l], a_ln_b[l], t["wkv"])
            w_o = a_w_o[l]
        else:
            j = l - n_a
            if j == 0:
                k_sh, v_sh = _kv_proj(x, kv_norm, bf(w_kv), qk_width, tables, t["qkv"])
            lam_init = 0.8 - 0.6 * math.exp(-0.3 * l)
            q = _q_proj(x, norm1[l], sh1, sc1, bf(b_w_q[j]), tables, t["qkv"])
            y = _diff_attn(q, k_sh, v_sh, b_lam[j], b_subln[j], lam_init, t["attn"])
            w_o = b_w_o[j]
        x = _ffn(x, y, bf(w_o), g1, norm2[l], sh2, sc2, g2, bf(ffn_w_up[l]), ffn_conv_w[l], ffn_conv_b[l],
                 bf(ffn_w_down[l]), final_norm if l == depth - 1 else None, t["ffn"])
    return x
```
